```python
import math
import jax, jax.numpy as jnp
from jax import lax
import numpy as np

D_MODEL = 2048
BATCH = 2
SEQ = 4096
DEPTH = 4
DEC_BATCH = 128
DEC_SEQ = 4
PAST_LEN = 8192
PAGE_SIZE = 128

HEAD_DIM = 128
A_HEADS = 8
A_DK = HEAD_DIM
A_DV = HEAD_DIM
A_WIDTH = A_HEADS * A_DV
CONV_W = 4
DELTA_CHUNK = 64
B_HEADS = 4
B_KV_HEADS = 1
B_GROUP = B_HEADS // B_KV_HEADS
B_WIDTH = B_HEADS * HEAD_DIM
MOBA_BLOCK = 256
MOBA_TOPK = 3
MOBA_Q_CHUNK = 64
C_HEADS = 4
C_NOPE = 128
C_ROPE = 64
C_V = 128
C_Q_RANK = 384
C_KV_RANK = 128
C_WIDTH = C_HEADS * C_V
ATTN_Q_BLOCK = 128

MIX_WIDTH = A_WIDTH + B_WIDTH + C_WIDTH
D_FF = ((8 * D_MODEL + 3 * 256 - 1) // (3 * 256)) * 256
ROPE_THETA = 10000.0
LN_EPS = 1e-5
RMS_EPS = 1e-6
L2_EPS = 1e-6
DEEPNORM_ALPHA = (2 * DEPTH) ** 0.25
DEEPNORM_BETA = (8 * DEPTH) ** -0.25
SPLIT_SIZES = (A_WIDTH, A_WIDTH, A_WIDTH, A_WIDTH, A_HEADS, A_HEADS,
               B_WIDTH, B_KV_HEADS * HEAD_DIM, B_KV_HEADS * HEAD_DIM,
               C_Q_RANK, C_KV_RANK, C_ROPE)
IN_WIDTH = 4 * A_WIDTH + 2 * A_HEADS + B_WIDTH + 2 * B_KV_HEADS * HEAD_DIM + C_Q_RANK + C_KV_RANK + C_ROPE

kernel_name = 'hybrid_delta_moba_mla_decoder_step'


def _layer_norm(x, g, b):
    xf = x.astype(jnp.float32)
    mu = jnp.mean(xf, axis=-1, keepdims=True)
    var = jnp.mean(jnp.square(xf - mu), axis=-1, keepdims=True)
    return ((xf - mu) * lax.rsqrt(var + LN_EPS) * g + b).astype(x.dtype)


def _rms_norm(x, g):
    xf = x.astype(jnp.float32)
    return (xf * lax.rsqrt(jnp.mean(xf * xf, axis=-1, keepdims=True) + RMS_EPS) * g).astype(x.dtype)


def _l2norm(x):
    xf = x.astype(jnp.float32)
    return xf * lax.rsqrt(jnp.sum(xf * xf, axis=-1, keepdims=True) + L2_EPS)


def _rope(x, pos):
    d = x.shape[-1]
    half = d // 2
    inv = ROPE_THETA ** (-jnp.arange(half, dtype=jnp.float32) * 2.0 / d)
    ang = pos.astype(jnp.float32)[:, None] * inv[None, :]
    cos = jnp.cos(ang)[:, None, :]
    sin = jnp.sin(ang)[:, None, :]
    xf = x.astype(jnp.float32)
    x1, x2 = xf[..., :half], xf[..., half:]
    return jnp.concatenate([x1 * cos - x2 * sin, x2 * cos + x1 * sin], axis=-1).astype(x.dtype)


def _gated_delta_rule(q, k, v, g, beta, s0):
    bsz, L, H, dk = q.shape
    dv = v.shape[-1]
    C = min(DELTA_CHUNK, L)
    n = -(-L // C)
    pad = n * C - L

    def chunk(t):
        t = jnp.pad(t, ((0, 0), (0, pad)) + ((0, 0),) * (t.ndim - 2))
        t = t.reshape((bsz, n, C) + t.shape[2:])
        return jnp.moveaxis(t, 3, 1)

    q, k, v, g, beta = chunk(q), chunk(k), chunk(v), chunk(g), chunk(beta)
    G = jnp.cumsum(g, axis=-1)
    incl = jnp.tril(jnp.ones((C, C), dtype=bool))
    strict = jnp.tril(jnp.ones((C, C), dtype=bool), -1)
    decay = jnp.exp(jnp.where(incl, G[..., :, None] - G[..., None, :], -jnp.inf))
    a_mat = jnp.where(strict, beta[..., :, None] * jnp.einsum('bhnid,bhnjd->bhnij', k, k) * decay, 0.0)
    lhs = a_mat + jnp.eye(C, dtype=jnp.float32)
    rhs = jnp.concatenate([beta[..., None] * v, (beta * jnp.exp(G))[..., None] * k], axis=-1)
    sol = lax.linalg.triangular_solve(lhs, rhs, left_side=True, lower=True, unit_diagonal=True)
    u, w = sol[..., :dv], sol[..., dv:]
    qk = jnp.einsum('bhnid,bhnjd->bhnij', q, k) * decay

    def step(S, xs):
        qc, kc, uc, wc, Gc, qkc = xs
        delta = uc - jnp.einsum('bhck,bhkv->bhcv', wc, S)
        o = (jnp.einsum('bhck,bhkv->bhcv', qc * jnp.exp(Gc)[..., None], S)
             + jnp.einsum('bhij,bhjv->bhiv', qkc, delta))
        g_last = Gc[..., -1]
        S = (S * jnp.exp(g_last)[..., None, None]
             + jnp.einsum('bhck,bhcv->bhkv', kc * jnp.exp(g_last[..., None] - Gc)[..., None], delta))
        return S, o

    xs = tuple(jnp.moveaxis(t, 2, 0) for t in (q, k, u, w, G, qk))
    s_final, o = lax.scan(step, s0, xs)
    o = jnp.moveaxis(jnp.moveaxis(o, 0, 2), 1, 3).reshape(bsz, n * C, H, dv)[:, :L]
    return o, s_final


def _delta_group(qa, ka, va, za, ba, aa, conv_buf, s0, w_conv, a_log, dt_bias, g_norm):
    bsz, L, _ = qa.shape
    u = jnp.concatenate([qa, ka, va], axis=-1)
    up = jnp.concatenate([conv_buf.astype(u.dtype), u], axis=1)
    y = up[:, 0:L] * w_conv[0]
    for i in range(1, CONV_W):
        y = y + up[:, i:i + L] * w_conv[i]
    y = jax.nn.silu(y)
    conv_new = up[:, -(CONV_W - 1):]
    q, k, v = jnp.split(y, 3, axis=-1)
    q = _l2norm(q.reshape(bsz, L, A_HEADS, A_DK)) * (A_DK ** -0.5)
    k = _l2norm(k.reshape(bsz, L, A_HEADS, A_DK))
    v = v.reshape(bsz, L, A_HEADS, A_DV).astype(jnp.float32)
    beta = jax.nn.sigmoid(ba.astype(jnp.float32))
    g = -jnp.exp(a_log.astype(jnp.float32)) * jax.nn.softplus(aa.astype(jnp.float32) + dt_bias.astype(jnp.float32))
    o, s_new = _gated_delta_rule(q, k, v, g, beta, s0.astype(jnp.float32))
    o = _rms_norm(o, g_norm) * jax.nn.silu(za.reshape(bsz, L, A_HEADS, A_DV).astype(jnp.float32))
    return o.reshape(bsz, L, A_WIDTH).astype(qa.dtype), conv_new, s_new.astype(qa.dtype)


def _moba_attention(q, k_all, v_all, q_pos, q_chunk):
    bsz, L, _, dh = q.shape
    T = k_all.shape[1]
    nb = max(-(-T // MOBA_BLOCK), MOBA_TOPK)
    pad_t = nb * MOBA_BLOCK - T

    def blocks(t):
        t = jnp.pad(t, ((0, 0), (0, pad_t), (0, 0), (0, 0)))
        return t.reshape(bsz, nb, MOBA_BLOCK, B_KV_HEADS, dh).transpose(0, 3, 1, 2, 4)

    k_blk, v_blk = blocks(k_all), blocks(v_all)
    k_mean = jnp.mean(k_blk.astype(jnp.float32), axis=3)
    n = -(-L // q_chunk)
    pad_l = n * q_chunk - L
    qp = jnp.pad(q, ((0, 0), (0, pad_l), (0, 0), (0, 0)))
    qp = qp.reshape(bsz, n, q_chunk, B_KV_HEADS, B_GROUP, dh).transpose(1, 0, 3, 4, 2, 5)
    pp = jnp.pad(q_pos, (0, pad_l), mode='edge').reshape(n, q_chunk)
    b_idx = jnp.arange(bsz)[:, None, None, None, None]
    h_idx = jnp.arange(B_KV_HEADS)[None, :, None, None, None]
    blk_ids = jnp.arange(nb)
    slot_ids = jnp.arange(MOBA_TOPK)
    in_blk = jnp.arange(MOBA_BLOCK)
    scale = dh ** -0.5

    def one(args):
        qq, p = args
        own = p // MOBA_BLOCK
        gate = jnp.einsum('bkgqd,bknd->bkgqn', qq.astype(jnp.float32), k_mean)
        gate = jnp.where(blk_ids[None, :] < own[:, None], gate, -jnp.inf)
        _, top = lax.top_k(gate, MOBA_TOPK)
        valid = slot_ids[None, :] < own[:, None]
        top = jnp.where(valid, top, 0)
        sel = jnp.concatenate([top, jnp.broadcast_to(own[:, None], top.shape[:-1] + (1,)).astype(top.dtype)], axis=-1)
        ok = jnp.concatenate([valid, jnp.ones((valid.shape[0], 1), dtype=bool)], axis=-1)
        k_sel = k_blk[b_idx, h_idx, sel]
        v_sel = v_blk[b_idx, h_idx, sel]
        s = jnp.einsum('bkgqd,bkgqsjd->bkgqsj', qq, k_sel, preferred_element_type=jnp.float32) * scale
        k_pos = sel[..., None] * MOBA_BLOCK + in_blk
        mask = ok[:, :, None] & (k_pos <= p[:, None, None])
        s = jnp.where(mask, s, -jnp.inf)
        wts = jax.nn.softmax(s, axis=(-2, -1))
        return jnp.einsum('bkgqsj,bkgqsjd->bkgqd', wts, v_sel)

    out = lax.map(one, (qp, pp))
    out = out.transpose(1, 0, 4, 2, 3, 5).reshape(bsz, n * q_chunk, B_HEADS, dh)[:, :L]
    return out.astype(q.dtype)


def _mla_attention(q_lat, q_rope, ckv, krope, q_pos):
    bsz, L, H, R = q_lat.shape
    T = ckv.shape[1]
    qb = min(ATTN_Q_BLOCK, L)
    n = -(-L // qb)
    pad_l = n * qb - L

    def blk(t):
        t = jnp.pad(t, ((0, 0), (0, pad_l), (0, 0), (0, 0)))
        return jnp.moveaxis(t.reshape(bsz, n, qb, H, t.shape[-1]), 1, 0)

    pp = jnp.pad(q_pos, (0, pad_l), mode='edge').reshape(n, qb)
    k_pos = jnp.arange(T)
    scale = (C_NOPE + C_ROPE) ** -0.5

    def one(args):
        ql, qr, p = args
        s = (jnp.einsum('bqhr,btr->bhqt', ql, ckv, preferred_element_type=jnp.float32)
             + jnp.einsum('bqhd,btd->bhqt', qr, krope, preferred_element_type=jnp.float32)) * scale
        s = jnp.where(k_pos[None, :] <= p[:, None], s, -jnp.inf)
        wts = jax.nn.softmax(s, axis=-1)
        return jnp.einsum('bhqt,btr->bqhr', wts, ckv)

    o = lax.map(one, (blk(q_lat), blk(q_rope), pp))
    return jnp.moveaxis(o, 0, 1).reshape(bsz, n * qb, H, R)[:, :L]


def _layer(x, pos, k_past, v_past, ckv_past, kr_past, conv_buf, s0, moba_q_chunk,
           w_in, w_conv, a_log, dt_bias, g_norm_a, g_q, w_uq, g_kv, w_uk, w_uv,
           w_o, ln1_g, ln1_b, w_ffn_in, w_ffn_out, ln2_g, ln2_b):
    bsz, L, _ = x.shape
    h = jnp.einsum('bld,de->ble', x, w_in)
    offs = [int(o) for o in np.cumsum(SPLIT_SIZES)[:-1]]
    qa, ka, va, za, ba, aa, qb, kb, vb, cq, ckv_raw, kr_raw = jnp.split(h, offs, axis=-1)
    ya, conv_new, s_new = _delta_group(qa, ka, va, za, ba, aa, conv_buf, s0, w_conv, a_log, dt_bias, g_norm_a)
    qb = _rope(qb.reshape(bsz, L, B_HEADS, HEAD_DIM), pos)
    kb = _rope(kb.reshape(bsz, L, B_KV_HEADS, HEAD_DIM), pos)
    vb = vb.reshape(bsz, L, B_KV_HEADS, HEAD_DIM)
    yb = _moba_attention(qb, jnp.concatenate([k_past.astype(kb.dtype), kb], axis=1),
                         jnp.concatenate([v_past.astype(vb.dtype), vb], axis=1), pos, moba_q_chunk)
    yb = yb.reshape(bsz, L, B_WIDTH).astype(x.dtype)
    q_full = jnp.einsum('blr,rhd->blhd', _rms_norm(cq, g_q), w_uq)
    q_nope = q_full[..., :C_NOPE]
    q_rope = _rope(q_full[..., C_NOPE:], pos)
    q_lat = jnp.einsum('blhd,rhd->blhr', q_nope, w_uk)
    ckv = _rms_norm(ckv_raw, g_kv)
    krope = _rope(kr_raw[:, :, None, :], pos)[:, :, 0]
    o_lat = _mla_attention(q_lat, q_rope, jnp.concatenate([ckv_past.astype(ckv.dtype), ckv], axis=1),
                           jnp.concatenate([kr_past.astype(krope.dtype), krope], axis=1), pos)
    yc = jnp.einsum('blhr,rhv->blhv', o_lat, w_uv).reshape(bsz, L, C_WIDTH).astype(x.dtype)
    mix = jnp.einsum('ble,ed->bld', jnp.concatenate([ya, yb, yc], axis=-1), w_o)
    x = _layer_norm(DEEPNORM_ALPHA * x + mix, ln1_g, ln1_b)
    gate, up = jnp.split(jnp.einsum('bld,df->blf', x, w_ffn_in), 2, axis=-1)
    f = jnp.einsum('blf,fd->bld', jax.nn.silu(gate) * up, w_ffn_out)
    x = _layer_norm(DEEPNORM_ALPHA * x + f, ln2_g, ln2_b)
    return x, (kb, vb, ckv, krope, s_new, conv_new)


def _gather_pages(pool, page_table):
    rows = pool[page_table]
    return rows.reshape((rows.shape[0], rows.shape[1] * rows.shape[2]) + rows.shape[3:])


def setup_inputs(seed: int = 0) -> dict:
    key = jax.random.key(seed)
    ks = jax.random.split(key, 32)
    nrm = jax.random.normal
    n_pages = PAST_LEN // PAGE_SIZE
    n_pool = (DEC_BATCH * n_pages * 5) // 4
    x_prompt = nrm(ks[0], (BATCH, SEQ, D_MODEL), jnp.float32)
    x_sample = nrm(ks[1], (DEC_BATCH, DEC_SEQ, D_MODEL), jnp.float32)
    cache_moba_k = nrm(ks[2], (DEPTH, n_pool, PAGE_SIZE, B_KV_HEADS, HEAD_DIM), jnp.float32)
    cache_moba_v = nrm(ks[3], (DEPTH, n_pool, PAGE_SIZE, B_KV_HEADS, HEAD_DIM), jnp.float32)
    cache_mla_ckv = nrm(ks[4], (DEPTH, n_pool, PAGE_SIZE, C_KV_RANK), jnp.float32)
    cache_mla_krope = nrm(ks[5], (DEPTH, n_pool, PAGE_SIZE, C_ROPE), jnp.float32)
    state_delta = nrm(ks[6], (DEPTH, DEC_BATCH, A_HEADS, A_DK, A_DV), jnp.float32) * (A_DK ** -0.5)
    state_conv = nrm(ks[7], (DEPTH, DEC_BATCH, CONV_W - 1, 3 * A_WIDTH), jnp.float32)
    page_table = jax.random.permutation(ks[8], n_pool)[: DEC_BATCH * n_pages].reshape(DEC_BATCH, n_pages).astype(jnp.int32)
    w_in = nrm(ks[9], (DEPTH, D_MODEL, IN_WIDTH), jnp.float32) * (D_MODEL ** -0.5)
    w_conv = nrm(ks[10], (DEPTH, CONV_W, 3 * A_WIDTH), jnp.float32) * (CONV_W ** -0.5)
    a_log = jnp.log(jax.random.uniform(ks[11], (DEPTH, A_HEADS), jnp.float32, 1.0, 16.0))
    dt = jnp.exp(jax.random.uniform(ks[12], (DEPTH, A_HEADS), jnp.float32, math.log(1e-3), math.log(1e-1)))
    dt_bias = dt + jnp.log(-jnp.expm1(-dt))
    g_norm_a = 1.0 + 0.02 * nrm(ks[13], (DEPTH, A_DV), jnp.float32)
    g_q = 1.0 + 0.02 * nrm(ks[14], (DEPTH, C_Q_RANK), jnp.float32)
    w_uq = nrm(ks[15], (DEPTH, C_Q_RANK, C_HEADS, C_NOPE + C_ROPE), jnp.float32) * (C_Q_RANK ** -0.5)
    g_kv = 1.0 + 0.02 * nrm(ks[16], (DEPTH, C_KV_RANK), jnp.float32)
    w_uk = nrm(ks[17], (DEPTH, C_KV_RANK, C_HEADS, C_NOPE), jnp.float32) * (C_KV_RANK ** -0.5)
    w_uv = nrm(ks[18], (DEPTH, C_KV_RANK, C_HEADS, C_V), jnp.float32) * (C_KV_RANK ** -0.5)
    w_o = nrm(ks[19], (DEPTH, MIX_WIDTH, D_MODEL), jnp.float32) * (MIX_WIDTH ** -0.5) * DEEPNORM_BETA
    ln1_g = 1.0 + 0.02 * nrm(ks[20], (DEPTH, D_MODEL), jnp.float32)
    ln1_b = 0.02 * nrm(ks[21], (DEPTH, D_MODEL), jnp.float32)
    w_ffn_in = nrm(ks[22], (DEPTH, D_MODEL, 2 * D_FF), jnp.float32) * (D_MODEL ** -0.5)
    w_ffn_out = nrm(ks[23], (DEPTH, D_FF, D_MODEL), jnp.float32) * (D_FF ** -0.5) * DEEPNORM_BETA
    ln2_g = 1.0 + 0.02 * nrm(ks[24], (DEPTH, D_MODEL), jnp.float32)
    ln2_b = 0.02 * nrm(ks[25], (DEPTH, D_MODEL), jnp.float32)
    return {'x_prompt': x_prompt, 'x_sample': x_sample,
            'cache_moba_k': cache_moba_k, 'cache_moba_v': cache_moba_v,
            'cache_mla_ckv': cache_mla_ckv, 'cache_mla_krope': cache_mla_krope,
            'state_delta': state_delta, 'state_conv': state_conv, 'page_table': page_table,
            'w_in': w_in, 'w_conv': w_conv, 'a_log': a_log, 'dt_bias': dt_bias, 'g_norm_a': g_norm_a,
            'g_q': g_q, 'w_uq': w_uq, 'g_kv': g_kv, 'w_uk': w_uk, 'w_uv': w_uv, 'w_o': w_o,
            'ln1_g': ln1_g, 'ln1_b': ln1_b, 'w_ffn_in': w_ffn_in, 'w_ffn_out': w_ffn_out,
            'ln2_g': ln2_g, 'ln2_b': ln2_b}


def reference(x_prompt, x_sample, cache_moba_k, cache_moba_v, cache_mla_ckv, cache_mla_krope,
              state_delta, state_conv, page_table, w_in, w_conv, a_log, dt_bias, g_norm_a,
              g_q, w_uq, g_kv, w_uk, w_uv, w_o, ln1_g, ln1_b, w_ffn_in, w_ffn_out, ln2_g, ln2_b):
    weights = (w_in, w_conv, a_log, dt_bias, g_norm_a, g_q, w_uq, g_kv, w_uk, w_uv,
               w_o, ln1_g, ln1_b, w_ffn_in, w_ffn_out, ln2_g, ln2_b)
    bp, lp, _ = x_prompt.shape
    pos_p = jnp.arange(lp, dtype=jnp.int32)
    h = x_prompt
    rec_p = []
    for l in range(DEPTH):
        h, new = _layer(h, pos_p,
                        jnp.zeros((bp, 0, B_KV_HEADS, HEAD_DIM), h.dtype),
                        jnp.zeros((bp, 0, B_KV_HEADS, HEAD_DIM), h.dtype),
                        jnp.zeros((bp, 0, C_KV_RANK), h.dtype),
                        jnp.zeros((bp, 0, C_ROPE), h.dtype),
                        jnp.zeros((bp, CONV_W - 1, 3 * A_WIDTH), h.dtype),
                        jnp.zeros((bp, A_HEADS, A_DK, A_DV), jnp.float32),
                        MOBA_Q_CHUNK, *[w[l] for w in weights])
        rec_p.append(new)
    y_prompt = h
    kp, vp, ckvp, krp, sdp, scp = [jnp.stack(t) for t in zip(*rec_p)]
    ls = x_sample.shape[1]
    n_past = page_table.shape[1] * PAGE_SIZE
    pos_s = n_past + jnp.arange(ls, dtype=jnp.int32)
    h = x_sample
    rec_s = []
    for l in range(DEPTH):
        h, new = _layer(h, pos_s,
                        _gather_pages(cache_moba_k[l], page_table),
                        _gather_pages(cache_moba_v[l], page_table),
                        _gather_pages(cache_mla_ckv[l], page_table),
                        _gather_pages(cache_mla_krope[l], page_table),
                        state_conv[l], state_delta[l],
                        1, *[w[l] for w in weights])
        rec_s.append(new)
    y_sample = h
    k_s, v_s, ckv_s, kr_s, sd_s, sc_s = [jnp.stack(t) for t in zip(*rec_s)]
    return (y_prompt, y_sample, kp, vp, ckvp, krp, sdp, scp, k_s, v_s, ckv_s, kr_s, sd_s, sc_s)
```

```python
import functools
import math

import jax
import jax.numpy as jnp
from jax import lax
from jax.experimental import pallas as pl
from jax.experimental.pallas import tpu as pltpu

F32 = jnp.float32
BF16 = jnp.bfloat16
HIGHEST = lax.Precision.HIGHEST

D_MODEL = 2048
PAGE_SIZE = 128
HEAD_DIM = 128
A_HEADS = 8
A_WIDTH = A_HEADS * HEAD_DIM
CONV_W = 4
DELTA_CHUNK = 64
B_HEADS = 4
B_WIDTH = B_HEADS * HEAD_DIM
MOBA_BLOCK = 256
MOBA_TOPK = 3
C_HEADS = 4
C_NOPE = 128
C_ROPE = 64
C_V = 128
C_Q_RANK = 384
C_KV_RANK = 128
C_WIDTH = C_HEADS * C_V
ROPE_THETA = 10000.0
LN_EPS = 1e-5
RMS_EPS = 1e-6
L2_EPS = 1e-6

OFF_QKV = 0
OFF_Z = 3 * A_WIDTH
OFF_QB = 4 * A_WIDTH
OFF_CQ = OFF_QB + B_WIDTH
OFF_KB = OFF_CQ + C_Q_RANK
OFF_VB = OFF_KB + HEAD_DIM
OFF_CKV = OFF_VB + HEAD_DIM
OFF_KR = OFF_CKV + C_KV_RANK
H_PAD = 5632
LANE_BETA = C_ROPE
LANE_DECAY = C_ROPE + A_HEADS

NEG = -1e30
VMEM_LIMIT = 56 * 1024 * 1024


def _cparams(sem):
    return pltpu.CompilerParams(dimension_semantics=sem, vmem_limit_bytes=VMEM_LIMIT)


def _dot(a, b):
    return jnp.dot(a, b, preferred_element_type=F32)


def _dot_exact(a, b):
    return jnp.dot(a, b, preferred_element_type=F32, precision=HIGHEST)


def _dot_nt(a, b, precision=None):
    return lax.dot_general(a, b, (((1,), (1,)), ((), ())), preferred_element_type=F32, precision=precision)


def _dot_tn(a, b, precision=None):
    return lax.dot_general(a, b, (((0,), (0,)), ((), ())), preferred_element_type=F32, precision=precision)


def _matmul_kernel(x_ref, w_ref, o_ref):
    o_ref[...] = _dot(x_ref[...], w_ref[...])


def _matmul(xb, w, layer, tm, tn):
    m, k = xb.shape
    n = w.shape[-1]
    return pl.pallas_call(
        _matmul_kernel,
        grid=(m // tm, n // tn),
        in_specs=[pl.BlockSpec((tm, k), lambda i, j: (i, 0)),
                  pl.BlockSpec((None, k, tn), lambda i, j: (layer, 0, j))],
        out_specs=pl.BlockSpec((tm, tn), lambda i, j: (i, j)),
        out_shape=jax.ShapeDtypeStruct((m, n), F32),
        compiler_params=_cparams(("parallel", "arbitrary")),
        name="proj_in",
    )(xb, w)


def _layer_norm_rows(y, g, b):
    mu = jnp.mean(y, axis=-1, keepdims=True)
    yc = y - mu
    var = jnp.mean(yc * yc, axis=-1, keepdims=True)
    return yc * lax.rsqrt(var + LN_EPS) * g + b


def _outproj_ln_kernel(mix_ref, w_ref, x_ref, g_ref, b_ref, o_ref, ob_ref, *, alpha):
    y = alpha * x_ref[...] + _dot(mix_ref[...], w_ref[...])
    out = _layer_norm_rows(y, g_ref[...], b_ref[...])
    o_ref[...] = out
    ob_ref[...] = out.astype(BF16)


def _outproj_ln(mix, w_o, x, g, b, layer, alpha, tm):
    m, d = x.shape
    kdim = mix.shape[1]
    return pl.pallas_call(
        functools.partial(_outproj_ln_kernel, alpha=alpha),
        grid=(m // tm,),
        in_specs=[pl.BlockSpec((tm, kdim), lambda i: (i, 0)),
                  pl.BlockSpec((None, kdim, d), lambda i: (layer, 0, 0)),
                  pl.BlockSpec((tm, d), lambda i: (i, 0)),
                  pl.BlockSpec((None, 1, d), lambda i: (layer, 0, 0)),
                  pl.BlockSpec((None, 1, d), lambda i: (layer, 0, 0))],
        out_specs=[pl.BlockSpec((tm, d), lambda i: (i, 0)),
                   pl.BlockSpec((tm, d), lambda i: (i, 0))],
        out_shape=[jax.ShapeDtypeStruct((m, d), F32), jax.ShapeDtypeStruct((m, d), BF16)],
        compiler_params=_cparams(("parallel",)),
        name="outproj_ln",
    )(mix, w_o, x, g, b)


def _ffn_ln_kernel(xb_ref, wg_ref, wu_ref, wo_ref, x_ref, g_ref, b_ref, o_ref, ob_ref, acc_ref, *, alpha):
    f = pl.program_id(1)

    @pl.when(f == 0)
    def _():
        acc_ref[...] = jnp.zeros_like(acc_ref)

    xb = xb_ref[...]
    gate = _dot(xb, wg_ref[...])
    up = _dot(xb, wu_ref[...])
    act = (jax.nn.silu(gate) * up).astype(BF16)
    acc_ref[...] += _dot(act, wo_ref[...])

    @pl.when(f == pl.num_programs(1) - 1)
    def _():
        y = alpha * x_ref[...] + acc_ref[...]
        out = _layer_norm_rows(y, g_ref[...], b_ref[...])
        o_ref[...] = out
        ob_ref[...] = out.astype(BF16)


def _ffn_ln(xb, x, w_in, w_out, g, b, layer, alpha, tm, tf):
    m, d = x.shape
    d_ff = w_out.shape[1]
    nf = d_ff // tf
    return pl.pallas_call(
        functools.partial(_ffn_ln_kernel, alpha=alpha),
        grid=(m // tm, nf),
        in_specs=[pl.BlockSpec((tm, d), lambda i, f: (i, 0)),
                  pl.BlockSpec((None, d, tf), lambda i, f: (layer, 0, f)),
                  pl.BlockSpec((None, d, tf), lambda i, f: (layer, 0, nf + f)),
                  pl.BlockSpec((None, tf, d), lambda i, f: (layer, f, 0)),
                  pl.BlockSpec((tm, d), lambda i, f: (i, 0)),
                  pl.BlockSpec((None, 1, d), lambda i, f: (layer, 0, 0)),
                  pl.BlockSpec((None, 1, d), lambda i, f: (layer, 0, 0))],
        out_specs=[pl.BlockSpec((tm, d), lambda i, f: (i, 0)),
                   pl.BlockSpec((tm, d), lambda i, f: (i, 0))],
        out_shape=[jax.ShapeDtypeStruct((m, d), F32), jax.ShapeDtypeStruct((m, d), BF16)],
        scratch_shapes=[pltpu.VMEM((tm, d), F32)],
        compiler_params=_cparams(("parallel", "arbitrary")),
        name="ffn_ln",
    )(xb, w_in, w_in, w_out, x, g, b)


def _swap_half64(a):
    lane = lax.broadcasted_iota(jnp.int32, a.shape, 1)
    first = jnp.bitwise_and(lane, 63) < 32
    return jnp.where(first, pltpu.roll(a, 96, 1), pltpu.roll(a, 32, 1))


def _prep_kernel(qb_ref, cq_ref, kb_ref, ckv_ref, kr_ref, cos_ref, sin_ref, cos64_ref, sin64_ref,
                 gq_ref, gkv_ref, wuq_ref, wukt_ref,
                 qrot_ref, krot_ref, ckvn_ref, krr_ref, qcat_ref, kcat_ref):
    cos = cos_ref[...]
    sin = sin_ref[...]
    q = qb_ref[...]
    for h in range(B_HEADS):
        xs = q[:, h * HEAD_DIM:(h + 1) * HEAD_DIM]
        qrot_ref[:, h * HEAD_DIM:(h + 1) * HEAD_DIM] = xs * cos + pltpu.roll(xs, HEAD_DIM // 2, 1) * sin
    k = kb_ref[...]
    krot_ref[...] = k * cos + pltpu.roll(k, HEAD_DIM // 2, 1) * sin

    cq = cq_ref[...]
    cqn = cq * lax.rsqrt(jnp.mean(cq * cq, axis=-1, keepdims=True) + RMS_EPS) * gq_ref[...]
    qfull = _dot(cqn.astype(BF16), wuq_ref[...])
    c64 = cos64_ref[...]
    s64 = sin64_ref[...]
    nope_w = C_HEADS * C_NOPE
    halves = []
    for half in range(2):
        a = qfull[:, nope_w + half * 128: nope_w + (half + 1) * 128]
        halves.append(a * c64[:, half * 128:(half + 1) * 128] + _swap_half64(a) * s64[:, half * 128:(half + 1) * 128])
    scale = (C_NOPE + C_ROPE) ** -0.5
    zpad = jnp.zeros((q.shape[0], 64), F32)
    for h in range(C_HEADS):
        ql = _dot(qfull[:, h * C_NOPE:(h + 1) * C_NOPE].astype(BF16), wukt_ref[h])
        hr = halves[h // 2][:, (h % 2) * C_ROPE:(h % 2 + 1) * C_ROPE]
        qcat_ref[:, h * 256:(h + 1) * 256] = (jnp.concatenate([ql, hr, zpad], axis=1) * scale).astype(BF16)

    ck = ckv_ref[...]
    ckn = ck * lax.rsqrt(jnp.mean(ck * ck, axis=-1, keepdims=True) + RMS_EPS) * gkv_ref[...]
    ckvn_ref[...] = ckn
    krb = kr_ref[...]
    krr = (krb * c64[:, :128] + _swap_half64(krb) * s64[:, :128])[:, :C_ROPE]
    krr_ref[...] = krr
    kcat_ref[...] = jnp.concatenate([ckn, krr, zpad], axis=1).astype(BF16)


def _prep(h, tabs, g_q, g_kv, w_uq, w_ukt, layer, tm):
    n = h.shape[0]
    cos128, sin128, cos64, sin64 = tabs
    row = lambda w, off: pl.BlockSpec((tm, w), lambda i: (i, off // w))
    tab = lambda w: pl.BlockSpec((tm, w), lambda i: (i, 0))
    return pl.pallas_call(
        _prep_kernel,
        grid=(n // tm,),
        in_specs=[row(B_WIDTH, OFF_QB), row(C_Q_RANK, OFF_CQ), row(HEAD_DIM, OFF_KB),
                  row(C_KV_RANK, OFF_CKV), row(128, OFF_KR),
                  tab(128), tab(128), tab(256), tab(256),
                  pl.BlockSpec((None, 1, C_Q_RANK), lambda i: (layer, 0, 0)),
                  pl.BlockSpec((None, 1, C_KV_RANK), lambda i: (layer, 0, 0)),
                  pl.BlockSpec((None, C_Q_RANK, 768), lambda i: (layer, 0, 0)),
                  pl.BlockSpec((None, C_HEADS, C_NOPE, C_KV_RANK), lambda i: (layer, 0, 0, 0))],
        out_specs=[tab(B_WIDTH), tab(HEAD_DIM), tab(C_KV_RANK), tab(C_ROPE), tab(1024), tab(256)],
        out_shape=[jax.ShapeDtypeStruct((n, B_WIDTH), F32),
                   jax.ShapeDtypeStruct((n, HEAD_DIM), F32),
                   jax.ShapeDtypeStruct((n, C_KV_RANK), F32),
                   jax.ShapeDtypeStruct((n, C_ROPE), F32),
                   jax.ShapeDtypeStruct((n, 1024), BF16),
                   jax.ShapeDtypeStruct((n, 256), BF16)],
        compiler_params=_cparams(("parallel",)),
        name="prep",
    )(h, h, h, h, h, cos128, sin128, cos64, sin64, g_q, g_kv, w_uq, w_ukt)


def _inv_unit_lower(a, c):
    r = lax.broadcasted_iota(jnp.int32, (c, c), 0)
    col = lax.broadcasted_iota(jnp.int32, (c, c), 1)
    eye = (r == col).astype(F32)
    a8 = jnp.where(jnp.right_shift(r, 3) == jnp.right_shift(col, 3), a, 0.0)
    x = eye - a8
    p = _dot_exact(a8, a8)
    x = x + _dot_exact(x, p)
    p = _dot_exact(p, p)
    x = x + _dot_exact(x, p)
    k = 8
    while k < c:
        sh = k.bit_length() - 1
        same2k = jnp.right_shift(r, sh + 1) == jnp.right_shift(col, sh + 1)
        samek = jnp.right_shift(r, sh) == jnp.right_shift(col, sh)
        m = jnp.where(same2k & jnp.logical_not(samek), a, 0.0)
        x = x - _dot_exact(_dot_exact(x, m), x)
        k *= 2
    return x


def _wy_chunk(q, k, v, g_col, g_row, beta_col, s):
    c = q.shape[0]
    r = lax.broadcasted_iota(jnp.int32, (c, c), 0)
    col = lax.broadcasted_iota(jnp.int32, (c, c), 1)
    decay = jnp.exp(jnp.where(r >= col, g_col - g_row, -jnp.inf))
    kb = k.astype(BF16)
    kk = _dot_nt(kb, kb)
    qk = _dot_nt(q.astype(BF16), kb)
    a = jnp.where(r > col, beta_col * kk * decay, 0.0)
    tinv = _inv_unit_lower(a, c)
    eg = jnp.exp(g_col)
    rhs = jnp.concatenate([beta_col * v, (beta_col * eg) * k], axis=1)
    sol = _dot_exact(tinv, rhs)
    u = sol[:, :HEAD_DIM]
    w = sol[:, HEAD_DIM:]
    sb = s.astype(BF16)
    delta = u - _dot(w.astype(BF16), sb)
    db = delta.astype(BF16)
    o = _dot((q * eg).astype(BF16), sb) + _dot((qk * decay).astype(BF16), db)
    g_last = g_col[c - 1:c, :]
    s_new = s * jnp.exp(g_last) + _dot_tn((k * jnp.exp(g_last - g_col)).astype(BF16), db)
    return o, s_new


def _l2norm_rows(x):
    return x * lax.rsqrt(jnp.sum(x * x, axis=-1, keepdims=True) + L2_EPS)


def _decay_and_beta(bb, alog_ref, dtb_ref, valid):
    lane = lax.broadcasted_iota(jnp.int32, bb.shape, 1)
    is_decay = (lane >= LANE_DECAY) & (lane < LANE_DECAY + A_HEADS)
    g = -jnp.exp(alog_ref[...]) * jax.nn.softplus(bb + dtb_ref[...])
    g = jnp.where(is_decay & valid, g, 0.0)
    beta = jnp.where(valid, jax.nn.sigmoid(bb), 0.0)
    return g, beta


def _gated_out(o, z, gnorm):
    on = o * lax.rsqrt(jnp.mean(o * o, axis=-1, keepdims=True) + RMS_EPS) * gnorm
    return (on * jax.nn.silu(z)).astype(BF16)


def _delta_prompt_kernel(u_ref, z_ref, ba_ref, wconv_ref, alog_ref, dtb_ref, gnorm_ref,
                         ya_ref, sfin_ref, cfin_ref,
                         ubuf, s_ref, qn_ref, kn_ref, vn_ref, o_ref, g_ref, gt_ref, beta_ref, *, tile, chunk):
    t = pl.program_id(1)
    nchunk = tile // chunk

    @pl.when(t == 0)
    def _():
        ubuf[0:8, :] = jnp.zeros((8, ubuf.shape[1]), F32)
        s_ref[...] = jnp.zeros_like(s_ref)

    ubuf[8:8 + tile, :] = u_ref[...]
    for grp in range(3 * A_HEADS):
        lo = grp * HEAD_DIM
        y = ubuf[5:5 + tile, lo:lo + HEAD_DIM] * wconv_ref[0:1, lo:lo + HEAD_DIM]
        for i in range(1, CONV_W):
            y = y + ubuf[5 + i:5 + i + tile, lo:lo + HEAD_DIM] * wconv_ref[i:i + 1, lo:lo + HEAD_DIM]
        y = jax.nn.silu(y)
        hh = grp % A_HEADS
        if grp < A_HEADS:
            qn_ref[hh] = _l2norm_rows(y) * (HEAD_DIM ** -0.5)
        elif grp < 2 * A_HEADS:
            kn_ref[hh] = _l2norm_rows(y)
        else:
            vn_ref[hh] = y
    cfin_ref[...] = ubuf[tile + 5:tile + 8, :]
    ubuf[0:8, :] = ubuf[tile:tile + 8, :]

    g, beta = _decay_and_beta(ba_ref[...], alog_ref, dtb_ref, True)
    beta_ref[...] = beta
    r = lax.broadcasted_iota(jnp.int32, (tile, tile), 0)
    col = lax.broadcasted_iota(jnp.int32, (tile, tile), 1)
    sh = chunk.bit_length() - 1
    same = jnp.right_shift(r, sh) == jnp.right_shift(col, sh)
    lower = jnp.where(same & (r >= col), 1.0, 0.0)
    upper = jnp.where(same & (r <= col), 1.0, 0.0)
    g_ref[...] = _dot_exact(lower, g)
    gt = _dot_exact(g.T, upper)
    for cidx in range(nchunk):
        gt_ref[cidx] = gt[:, cidx * chunk:(cidx + 1) * chunk]

    def body(cidx, carry):
        r0 = pl.multiple_of(cidx * chunk, chunk)
        g_all = g_ref[pl.ds(r0, chunk), :]
        b_all = beta_ref[pl.ds(r0, chunk), :]
        gt_all = gt_ref[cidx]
        for hh in range(A_HEADS):
            o, s_new = _wy_chunk(qn_ref[hh, pl.ds(r0, chunk), :], kn_ref[hh, pl.ds(r0, chunk), :],
                                 vn_ref[hh, pl.ds(r0, chunk), :],
                                 g_all[:, LANE_DECAY + hh:LANE_DECAY + hh + 1],
                                 gt_all[LANE_DECAY + hh:LANE_DECAY + hh + 1, :],
                                 b_all[:, LANE_BETA + hh:LANE_BETA + hh + 1],
                                 s_ref[hh])
            s_ref[hh] = s_new
            o_ref[hh, pl.ds(r0, chunk), :] = o
        return carry

    lax.fori_loop(0, nchunk, body, 0)

    gnorm = gnorm_ref[...]
    for hh in range(A_HEADS):
        lo = hh * HEAD_DIM
        ya_ref[:, lo:lo + HEAD_DIM] = _gated_out(o_ref[hh], z_ref[:, lo:lo + HEAD_DIM], gnorm)

    @pl.when(t == pl.num_programs(1) - 1)
    def _():
        sfin_ref[...] = s_ref[...]


def _delta_prompt(h, w_conv, alog_vec, dtb_vec, g_norm, layer, bp, lp, tile):
    nt = lp // tile
    qkv_w = 3 * A_WIDTH
    kern = functools.partial(_delta_prompt_kernel, tile=tile, chunk=DELTA_CHUNK)
    return pl.pallas_call(
        kern,
        grid=(bp, nt),
        in_specs=[pl.BlockSpec((tile, qkv_w), lambda b, t: (b * nt + t, 0)),
                  pl.BlockSpec((tile, A_WIDTH), lambda b, t: (b * nt + t, OFF_Z // A_WIDTH)),
                  pl.BlockSpec((tile, 128), lambda b, t: (b * nt + t, OFF_KR // 128)),
                  pl.BlockSpec((None, CONV_W, qkv_w), lambda b, t: (layer, 0, 0)),
                  pl.BlockSpec((None, 1, 128), lambda b, t: (layer, 0, 0)),
                  pl.BlockSpec((None, 1, 128), lambda b, t: (layer, 0, 0)),
                  pl.BlockSpec((None, 1, HEAD_DIM), lambda b, t: (layer, 0, 0))],
        out_specs=[pl.BlockSpec((tile, A_WIDTH), lambda b, t: (b * nt + t, 0)),
                   pl.BlockSpec((None, A_HEADS, HEAD_DIM, HEAD_DIM), lambda b, t: (b, 0, 0, 0)),
                   pl.BlockSpec((None, CONV_W - 1, qkv_w), lambda b, t: (b, 0, 0))],
        out_shape=[jax.ShapeDtypeStruct((bp * lp, A_WIDTH), BF16),
                   jax.ShapeDtypeStruct((bp, A_HEADS, HEAD_DIM, HEAD_DIM), F32),
                   jax.ShapeDtypeStruct((bp, CONV_W - 1, qkv_w), F32)],
        scratch_shapes=[pltpu.VMEM((tile + 8, qkv_w), F32),
                        pltpu.VMEM((A_HEADS, HEAD_DIM, HEAD_DIM), F32),
                        pltpu.VMEM((A_HEADS, tile, HEAD_DIM), F32),
                        pltpu.VMEM((A_HEADS, tile, HEAD_DIM), F32),
                        pltpu.VMEM((A_HEADS, tile, HEAD_DIM), F32),
                        pltpu.VMEM((A_HEADS, tile, HEAD_DIM), F32),
                        pltpu.VMEM((tile, 128), F32),
                        pltpu.VMEM((tile // DELTA_CHUNK, 128, DELTA_CHUNK), F32),
                        pltpu.VMEM((tile, 128), F32)],
        compiler_params=_cparams(("parallel", "arbitrary")),
        name="delta_prompt",
    )(h, h, h, w_conv, alog_vec, dtb_vec, g_norm)


def _delta_sample_kernel(u_ref, z_ref, ba_ref, cs_ref, s0_ref, wconv_ref, alog_ref, dtb_ref, gnorm_ref,
                         ya_ref, snew_ref, cnew_ref, buf, *, ls):
    rows = 8
    width = buf.shape[1]
    buf[0:CONV_W - 1, :] = cs_ref[...]
    buf[CONV_W - 1:CONV_W - 1 + ls, :] = u_ref[...]
    buf[CONV_W - 1 + ls:, :] = jnp.zeros((buf.shape[0] - (CONV_W - 1 + ls), width), F32)
    cnew_ref[...] = buf[ls:ls + CONV_W - 1, :]

    rid = lax.broadcasted_iota(jnp.int32, (rows, 128), 0)
    bb = jnp.concatenate([ba_ref[...], jnp.zeros((rows - ls, 128), F32)], axis=0)
    g, beta = _decay_and_beta(bb, alog_ref, dtb_ref, rid < ls)
    r = lax.broadcasted_iota(jnp.int32, (rows, rows), 0)
    col = lax.broadcasted_iota(jnp.int32, (rows, rows), 1)
    g_cum = _dot_exact(jnp.where(r >= col, 1.0, 0.0), g)
    gt_cum = _dot_tn(g, jnp.where(r <= col, 1.0, 0.0), precision=HIGHEST)

    def conv_group(grp):
        lo = grp * HEAD_DIM
        y = buf[0:rows, lo:lo + HEAD_DIM] * wconv_ref[0:1, lo:lo + HEAD_DIM]
        for i in range(1, CONV_W):
            y = y + buf[i:i + rows, lo:lo + HEAD_DIM] * wconv_ref[i:i + 1, lo:lo + HEAD_DIM]
        return jax.nn.silu(y)

    gnorm = gnorm_ref[...]
    for hh in range(A_HEADS):
        q = _l2norm_rows(conv_group(hh)) * (HEAD_DIM ** -0.5)
        k = _l2norm_rows(conv_group(A_HEADS + hh))
        v = conv_group(2 * A_HEADS + hh)
        o, s_new = _wy_chunk(q, k, v,
                             g_cum[:, LANE_DECAY + hh:LANE_DECAY + hh + 1],
                             gt_cum[LANE_DECAY + hh:LANE_DECAY + hh + 1, :],
                             beta[:, LANE_BETA + hh:LANE_BETA + hh + 1],
                             s0_ref[hh])
        snew_ref[hh] = s_new
        lo = hh * HEAD_DIM
        ya_ref[:, lo:lo + HEAD_DIM] = _gated_out(o[0:ls, :], z_ref[:, lo:lo + HEAD_DIM], gnorm)


def _delta_sample(h3, state_conv, state_delta, w_conv, alog_vec, dtb_vec, g_norm, layer, row0, bs, ls):
    qkv_w = 3 * A_WIDTH
    kern = functools.partial(_delta_sample_kernel, ls=ls)
    return pl.pallas_call(
        kern,
        grid=(bs,),
        in_specs=[pl.BlockSpec((None, ls, qkv_w), lambda b: (row0 + b, 0, 0)),
                  pl.BlockSpec((None, ls, A_WIDTH), lambda b: (row0 + b, 0, OFF_Z // A_WIDTH)),
                  pl.BlockSpec((None, ls, 128), lambda b: (row0 + b, 0, OFF_KR // 128)),
                  pl.BlockSpec((None, None, CONV_W - 1, qkv_w), lambda b: (layer, b, 0, 0)),
                  pl.BlockSpec((None, None, A_HEADS, HEAD_DIM, HEAD_DIM), lambda b: (layer, b, 0, 0, 0)),
                  pl.BlockSpec((None, CONV_W, qkv_w), lambda b: (layer, 0, 0)),
                  pl.BlockSpec((None, 1, 128), lambda b: (layer, 0, 0)),
                  pl.BlockSpec((None, 1, 128), lambda b: (layer, 0, 0)),
                  pl.BlockSpec((None, 1, HEAD_DIM), lambda b: (layer, 0, 0))],
        out_specs=[pl.BlockSpec((None, ls, A_WIDTH), lambda b: (b, 0, 0)),
                   pl.BlockSpec((None, A_HEADS, HEAD_DIM, HEAD_DIM), lambda b: (b, 0, 0, 0)),
                   pl.BlockSpec((None, CONV_W - 1, qkv_w), lambda b: (b, 0, 0))],
        out_shape=[jax.ShapeDtypeStruct((bs, ls, A_WIDTH), BF16),
                   jax.ShapeDtypeStruct((bs, A_HEADS, HEAD_DIM, HEAD_DIM), F32),
                   jax.ShapeDtypeStruct((bs, CONV_W - 1, qkv_w), F32)],
        scratch_shapes=[pltpu.VMEM((16, qkv_w), F32)],
        compiler_params=_cparams(("parallel",)),
        name="delta_sample",
    )(h3, h3, h3, state_conv, state_delta, w_conv, alog_vec, dtb_vec, g_norm)


def _top_blocks(gate, n_valid):
    blk = lax.broadcasted_iota(jnp.int32, gate.shape, 1)
    big = jnp.int32(2 ** 30)
    cand = blk < n_valid
    g = jnp.where(cand, gate, -jnp.inf)
    picks = []
    for _ in range(MOBA_TOPK):
        mx = jnp.max(g, axis=1, keepdims=True)
        first = jnp.min(jnp.where((g == mx) & cand, blk, big), axis=1, keepdims=True)
        picks.append(jnp.where(first == big, -1, first))
        hit = blk == first
        cand = cand & jnp.logical_not(hit)
        g = jnp.where(hit, -jnp.inf, g)
    return picks


def _stack_heads(x, heads, width):
    return jnp.concatenate([x[:, h * width:(h + 1) * width] for h in range(heads)], axis=0)


def _unstack_heads(x, heads, rows):
    return jnp.concatenate([x[h * rows:(h + 1) * rows, :] for h in range(heads)], axis=1)


def _moba_prompt_kernel(q_ref, k_ref, v_ref, o_ref, kmean_ref, m_ref, l_ref, acc_ref, *, nblk):
    i = pl.program_id(1)
    tq = MOBA_BLOCK
    scale = HEAD_DIM ** -0.5

    @pl.when(i == 0)
    def _():
        kmean_ref[...] = jnp.zeros_like(kmean_ref)
        for j in range(nblk):
            kmean_ref[j:j + 1, :] = jnp.mean(k_ref[j * tq:(j + 1) * tq, :], axis=0, keepdims=True)

    qs = _stack_heads(q_ref[...], B_HEADS, HEAD_DIM)
    qb = qs.astype(BF16)
    gate = _dot_nt(qs, kmean_ref[...], precision=HIGHEST)
    picks = _top_blocks(gate, i)

    r0 = pl.multiple_of(i * tq, tq)
    kd = k_ref[pl.ds(r0, tq), :].astype(BF16)
    vd = v_ref[pl.ds(r0, tq), :].astype(BF16)
    s = _dot_nt(qb, kd) * scale
    qpos = jnp.bitwise_and(lax.broadcasted_iota(jnp.int32, s.shape, 0), tq - 1)
    kpos = lax.broadcasted_iota(jnp.int32, s.shape, 1)
    s = jnp.where(kpos <= qpos, s, NEG)
    m0 = jnp.max(s, axis=1, keepdims=True)
    p = jnp.exp(s - m0)
    m_ref[...] = m0
    l_ref[...] = jnp.sum(p, axis=1, keepdims=True)
    acc_ref[...] = _dot(p.astype(BF16), vd)

    def body(j, carry):
        c0 = pl.multiple_of(j * tq, tq)
        kj = k_ref[pl.ds(c0, tq), :].astype(BF16)
        vj = v_ref[pl.ds(c0, tq), :].astype(BF16)
        chosen = (picks[0] == j) | (picks[1] == j) | (picks[2] == j)
        sj = jnp.where(chosen, _dot_nt(qb, kj) * scale, NEG)
        m_old = m_ref[...]
        m_new = jnp.maximum(m_old, jnp.max(sj, axis=1, keepdims=True))
        alpha = jnp.exp(m_old - m_new)
        pj = jnp.exp(sj - m_new)
        l_ref[...] = alpha * l_ref[...] + jnp.sum(pj, axis=1, keepdims=True)
        acc_ref[...] = alpha * acc_ref[...] + _dot(pj.astype(BF16), vj)
        m_ref[...] = m_new
        return carry

    lax.fori_loop(0, i, body, 0)
    out = acc_ref[...] / l_ref[...]
    o_ref[...] = _unstack_heads(out, B_HEADS, tq).astype(BF16)


def _moba_prompt(q_rot, k_rot, v, bp, lp):
    tq = MOBA_BLOCK
    nq = lp // tq
    nblk = lp // MOBA_BLOCK
    nblk_pad = -(-nblk // 8) * 8
    kern = functools.partial(_moba_prompt_kernel, nblk=nblk)
    return pl.pallas_call(
        kern,
        grid=(bp, nq),
        in_specs=[pl.BlockSpec((tq, B_WIDTH), lambda b, i: (b * nq + i, 0)),
                  pl.BlockSpec((lp, HEAD_DIM), lambda b, i: (b, 0)),
                  pl.BlockSpec((lp, HEAD_DIM), lambda b, i: (b, 0))],
        out_specs=pl.BlockSpec((tq, B_WIDTH), lambda b, i: (b * nq + i, 0)),
        out_shape=jax.ShapeDtypeStruct((bp * lp, B_WIDTH), BF16),
        scratch_shapes=[pltpu.VMEM((nblk_pad, HEAD_DIM), F32),
                        pltpu.VMEM((B_HEADS * tq, 1), F32),
                        pltpu.VMEM((B_HEADS * tq, 1), F32),
                        pltpu.VMEM((B_HEADS * tq, HEAD_DIM), F32)],
        compiler_params=_cparams(("parallel", "arbitrary")),
        name="moba_prompt",
    )(q_rot, k_rot, v)


def _mla_prompt_kernel(q_ref, kc_ref, wuv_ref, o_ref, m_ref, l_ref, acc_ref, *, tq):
    i = pl.program_id(1)
    qs = _stack_heads(q_ref[...], C_HEADS, 256)

    r0 = pl.multiple_of(i * tq, tq)
    kd = kc_ref[pl.ds(r0, tq), :]
    s = _dot_nt(qs, kd)
    qpos = jnp.bitwise_and(lax.broadcasted_iota(jnp.int32, s.shape, 0), tq - 1)
    kpos = lax.broadcasted_iota(jnp.int32, s.shape, 1)
    s = jnp.where(kpos <= qpos, s, NEG)
    m0 = jnp.max(s, axis=1, keepdims=True)
    p = jnp.exp(s - m0)
    m_ref[...] = m0
    l_ref[...] = jnp.sum(p, axis=1, keepdims=True)
    acc_ref[...] = _dot(p.astype(BF16), kd[:, :C_KV_RANK])

    def body(j, carry):
        c0 = pl.multiple_of(j * tq, tq)
        kj = kc_ref[pl.ds(c0, tq), :]
        sj = _dot_nt(qs, kj)
        m_old = m_ref[...]
        m_new = jnp.maximum(m_old, jnp.max(sj, axis=1, keepdims=True))
        alpha = jnp.exp(m_old - m_new)
        pj = jnp.exp(sj - m_new)
        l_ref[...] = alpha * l_ref[...] + jnp.sum(pj, axis=1, keepdims=True)
        acc_ref[...] = alpha * acc_ref[...] + _dot(pj.astype(BF16), kj[:, :C_KV_RANK])
        m_ref[...] = m_new
        return carry

    lax.fori_loop(0, i, body, 0)
    o_lat = (acc_ref[...] / l_ref[...]).astype(BF16)
    for h in range(C_HEADS):
        o_ref[:, h * C_V:(h + 1) * C_V] = _dot(o_lat[h * tq:(h + 1) * tq, :], wuv_ref[h]).astype(BF16)


def _mla_prompt(qcat, kcat, w_uv, layer, bp, lp, tq):
    nq = lp // tq
    kern = functools.partial(_mla_prompt_kernel, tq=tq)
    return pl.pallas_call(
        kern,
        grid=(bp, nq),
        in_specs=[pl.BlockSpec((tq, 1024), lambda b, i: (b * nq + i, 0)),
                  pl.BlockSpec((lp, 256), lambda b, i: (b, 0)),
                  pl.BlockSpec((None, C_HEADS, C_KV_RANK, C_V), lambda b, i: (layer, 0, 0, 0))],
        out_specs=pl.BlockSpec((tq, C_WIDTH), lambda b, i: (b * nq + i, 0)),
        out_shape=jax.ShapeDtypeStruct((bp * lp, C_WIDTH), BF16),
        scratch_shapes=[pltpu.VMEM((C_HEADS * tq, 1), F32),
                        pltpu.VMEM((C_HEADS * tq, 1), F32),
                        pltpu.VMEM((C_HEADS * tq, C_KV_RANK), F32)],
        compiler_params=_cparams(("parallel", "arbitrary")),
        name="mla_prompt",
    )(qcat, kcat, w_uv)


def _attn_sample_kernel(pt_ref, qm_ref, knew_ref, vnew_ref, qc_ref, kcnew_ref, wuv_ref,
                        ck_hbm, cv_hbm, cc_hbm, cr_hbm,
                        yb_ref, yc_ref,
                        kbuf, vbuf, cbuf, rbuf, sems, kmean_ref, sm_ref, sc_ref,
                        *, n_pages, page_off, ls):
    b = pl.program_id(0)
    nbatch = pl.num_programs(0)
    slot = lax.rem(b, 2)
    n_past = n_pages * PAGE_SIZE
    nblk = n_past // MOBA_BLOCK
    rows = B_HEADS * ls

    def page_copies(bb, p, sl):
        pg = pt_ref[bb, p] + page_off
        dst = pl.ds(pl.multiple_of(p * PAGE_SIZE, PAGE_SIZE), PAGE_SIZE)
        return (pltpu.make_async_copy(ck_hbm.at[pg], kbuf.at[sl, dst, :], sems.at[0, sl]),
                pltpu.make_async_copy(cv_hbm.at[pg], vbuf.at[sl, dst, :], sems.at[1, sl]),
                pltpu.make_async_copy(cc_hbm.at[pg], cbuf.at[sl, dst, :], sems.at[2, sl]),
                pltpu.make_async_copy(cr_hbm.at[pg], rbuf.at[sl, dst, :], sems.at[3, sl]))

    def start_batch(bb, sl):
        def body(p, carry):
            for cp in page_copies(bb, p, sl):
                cp.start()
            return carry
        lax.fori_loop(0, n_pages, body, 0)

    def wait_batch(bb, sl):
        def body(p, carry):
            for cp in page_copies(bb, p, sl):
                cp.wait()
            return carry
        lax.fori_loop(0, n_pages, body, 0)

    @pl.when(b == 0)
    def _():
        start_batch(b, slot)

    @pl.when(b + 1 < nbatch)
    def _():
        start_batch(b + 1, 1 - slot)

    wait_batch(b, slot)

    rtok = lax.rem(lax.broadcasted_iota(jnp.int32, (rows, ls), 0), ls)
    ctok = lax.broadcasted_iota(jnp.int32, (rows, ls), 1)
    causal_new = ctok <= rtok

    kmean_ref[...] = jnp.zeros_like(kmean_ref)
    for j in range(nblk):
        kmean_ref[j:j + 1, :] = jnp.mean(kbuf[slot, j * MOBA_BLOCK:(j + 1) * MOBA_BLOCK, :], axis=0, keepdims=True)
    qs = _stack_heads(qm_ref[...], B_HEADS, HEAD_DIM)
    qb = qs.astype(BF16)
    gate = _dot_nt(qs, kmean_ref[...], precision=HIGHEST)
    picks = _top_blocks(gate, nblk)
    scale = HEAD_DIM ** -0.5
    for j in range(nblk):
        kj = kbuf[slot, j * MOBA_BLOCK:(j + 1) * MOBA_BLOCK, :].astype(BF16)
        chosen = (picks[0] == j) | (picks[1] == j) | (picks[2] == j)
        sm_ref[:, j * MOBA_BLOCK:(j + 1) * MOBA_BLOCK] = jnp.where(chosen, _dot_nt(qb, kj) * scale, NEG)
    s_new = jnp.where(causal_new, _dot_nt(qb, knew_ref[...].astype(BF16)) * scale, NEG)
    s_all = sm_ref[...]
    m = jnp.maximum(jnp.max(s_all, axis=1, keepdims=True), jnp.max(s_new, axis=1, keepdims=True))
    p_new = jnp.exp(s_new - m)
    sm_ref[...] = jnp.exp(s_all - m)
    den = jnp.sum(sm_ref[...], axis=1, keepdims=True) + jnp.sum(p_new, axis=1, keepdims=True)
    acc = _dot(p_new.astype(BF16), vnew_ref[...].astype(BF16))
    for j in range(nblk):
        vj = vbuf[slot, j * MOBA_BLOCK:(j + 1) * MOBA_BLOCK, :].astype(BF16)
        acc = acc + _dot(sm_ref[:, j * MOBA_BLOCK:(j + 1) * MOBA_BLOCK].astype(BF16), vj)
    yb_ref[...] = _unstack_heads(acc / den, B_HEADS, ls).astype(BF16)

    qc = _stack_heads(qc_ref[...], C_HEADS, 256)
    ql = qc[:, :C_KV_RANK]
    qr = qc[:, C_KV_RANK:C_KV_RANK + C_ROPE]
    step = MOBA_BLOCK
    for j in range(n_past // step):
        cj = cbuf[slot, j * step:(j + 1) * step, :].astype(BF16)
        rj = rbuf[slot, j * step:(j + 1) * step, :].astype(BF16)
        sc_ref[:, j * step:(j + 1) * step] = _dot_nt(ql, cj) + _dot_nt(qr, rj)
    kcn = kcnew_ref[...]
    s_new = jnp.where(causal_new, _dot_nt(qc, kcn), NEG)
    s_all = sc_ref[...]
    m = jnp.maximum(jnp.max(s_all, axis=1, keepdims=True), jnp.max(s_new, axis=1, keepdims=True))
    p_new = jnp.exp(s_new - m)
    sc_ref[...] = jnp.exp(s_all - m)
    den = jnp.sum(sc_ref[...], axis=1, keepdims=True) + jnp.sum(p_new, axis=1, keepdims=True)
    acc = _dot(p_new.astype(BF16), kcn[:, :C_KV_RANK])
    for j in range(n_past // step):
        cj = cbuf[slot, j * step:(j + 1) * step, :].astype(BF16)
        acc = acc + _dot(sc_ref[:, j * step:(j + 1) * step].astype(BF16), cj)
    o_lat = (acc / den).astype(BF16)
    for h in range(C_HEADS):
        yc_ref[:, h * C_V:(h + 1) * C_V] = _dot(o_lat[h * ls:(h + 1) * ls, :], wuv_ref[h]).astype(BF16)


def _attn_sample(page_table, q_rot3, k_rot3, v3, qcat3, kcat3, w_uv, caches, layer, n_pool, bs, ls):
    n_pages = page_table.shape[1]
    n_past = n_pages * PAGE_SIZE
    nblk = n_past // MOBA_BLOCK
    nblk_pad = -(-nblk // 8) * 8
    rows = B_HEADS * ls
    kern = functools.partial(_attn_sample_kernel, n_pages=n_pages, page_off=layer * n_pool, ls=ls)
    new = lambda w: pl.BlockSpec((None, ls, w), lambda b, pt: (b, 0, 0))
    grid_spec = pltpu.PrefetchScalarGridSpec(
        num_scalar_prefetch=1,
        grid=(bs,),
        in_specs=[new(B_WIDTH), new(HEAD_DIM), new(HEAD_DIM), new(1024), new(256),
                  pl.BlockSpec((None, C_HEADS, C_KV_RANK, C_V), lambda b, pt: (layer, 0, 0, 0)),
                  pl.BlockSpec(memory_space=pl.ANY), pl.BlockSpec(memory_space=pl.ANY),
                  pl.BlockSpec(memory_space=pl.ANY), pl.BlockSpec(memory_space=pl.ANY)],
        out_specs=[new(B_WIDTH), new(C_WIDTH)],
        scratch_shapes=[pltpu.VMEM((2, n_past, HEAD_DIM), F32),
                        pltpu.VMEM((2, n_past, HEAD_DIM), F32),
                        pltpu.VMEM((2, n_past, C_KV_RANK), F32),
                        pltpu.VMEM((2, n_past, C_ROPE), F32),
                        pltpu.SemaphoreType.DMA((4, 2)),
                        pltpu.VMEM((nblk_pad, HEAD_DIM), F32),
                        pltpu.VMEM((rows, n_past), F32),
                        pltpu.VMEM((rows, n_past), F32)])
    return pl.pallas_call(
        kern,
        grid_spec=grid_spec,
        out_shape=[jax.ShapeDtypeStruct((bs, ls, B_WIDTH), BF16),
                   jax.ShapeDtypeStruct((bs, ls, C_WIDTH), BF16)],
        compiler_params=_cparams(("arbitrary",)),
        name="attn_sample",
    )(page_table, q_rot3, k_rot3, v3, qcat3, kcat3, w_uv, *caches)


def _rope_tables(pos, d, reps):
    half = d // 2
    inv = ROPE_THETA ** (-jnp.arange(half, dtype=F32) * 2.0 / d)
    ang = pos.astype(F32)[:, None] * inv[None, :]
    cos = jnp.cos(ang)
    sin = jnp.sin(ang)
    return (jnp.tile(jnp.concatenate([cos, cos], axis=1), (1, reps)),
            jnp.tile(jnp.concatenate([-sin, sin], axis=1), (1, reps)))


def _lane_vec(v, lane0):
    depth, heads = v.shape
    out = jnp.zeros((depth, 1, 128), F32)
    return out.at[:, 0, lane0:lane0 + heads].set(v.astype(F32))


def kernel(x_prompt, x_sample, cache_moba_k, cache_moba_v, cache_mla_ckv, cache_mla_krope, state_delta, state_conv, page_table, w_in, w_conv, a_log, dt_bias, g_norm_a, g_q, w_uq, g_kv, w_uk, w_uv, w_o, ln1_g, ln1_b, w_ffn_in, w_ffn_out, ln2_g, ln2_b):
    bp, lp, d = x_prompt.shape
    bs, ls, _ = x_sample.shape
    depth = w_in.shape[0]
    n_pool = cache_moba_k.shape[1]
    n_pages = page_table.shape[1]
    n_past = n_pages * PAGE_SIZE
    n_p = bp * lp
    n_s = bs * ls
    n = n_p + n_s
    assert d == D_MODEL and lp % MOBA_BLOCK == 0 and n_past % MOBA_BLOCK == 0 and ls <= 8
    assert n_p % ls == 0
    alpha = (2 * depth) ** 0.25

    tm = math.gcd(math.gcd(n_p, n_s), 512)
    tile_a = 256 if lp % 256 == 0 else DELTA_CHUNK

    qa_end = 4 * A_WIDTH
    o_ba = qa_end
    o_qb = o_ba + 2 * A_HEADS
    o_kb = o_qb + B_WIDTH
    o_vb = o_kb + HEAD_DIM
    o_cq = o_vb + HEAD_DIM
    o_ckv = o_cq + C_Q_RANK
    o_kr = o_ckv + C_KV_RANK
    o_end = o_kr + C_ROPE
    used = OFF_KR + C_ROPE + 2 * A_HEADS
    w_in_p = jnp.concatenate(
        [w_in[..., :qa_end], w_in[..., o_qb:o_kb], w_in[..., o_cq:o_ckv], w_in[..., o_kb:o_vb],
         w_in[..., o_vb:o_cq], w_in[..., o_ckv:o_kr], w_in[..., o_kr:o_end], w_in[..., o_ba:o_qb],
         jnp.zeros(w_in.shape[:2] + (H_PAD - used,), w_in.dtype)], axis=-1).astype(BF16)
    w_uq_p = jnp.concatenate([w_uq[..., :C_NOPE].reshape(depth, C_Q_RANK, C_HEADS * C_NOPE),
                              w_uq[..., C_NOPE:].reshape(depth, C_Q_RANK, C_HEADS * C_ROPE)], axis=-1).astype(BF16)
    w_ukt = jnp.transpose(w_uk, (0, 2, 3, 1)).astype(BF16)
    w_uv_p = jnp.transpose(w_uv, (0, 2, 1, 3)).astype(BF16)
    w_o_b = w_o.astype(BF16)
    w_ffn_in_b = w_ffn_in.astype(BF16)
    w_ffn_out_b = w_ffn_out.astype(BF16)
    alog_vec = _lane_vec(a_log, LANE_DECAY)
    dtb_vec = _lane_vec(dt_bias, LANE_DECAY)
    g_norm3 = g_norm_a.reshape(depth, 1, HEAD_DIM)
    g_q3 = g_q.reshape(depth, 1, C_Q_RANK)
    g_kv3 = g_kv.reshape(depth, 1, C_KV_RANK)
    ln1_g3, ln1_b3 = ln1_g.reshape(depth, 1, d), ln1_b.reshape(depth, 1, d)
    ln2_g3, ln2_b3 = ln2_g.reshape(depth, 1, d), ln2_b.reshape(depth, 1, d)

    pos = jnp.concatenate([jnp.tile(jnp.arange(lp, dtype=jnp.int32), bp),
                           jnp.tile(n_past + jnp.arange(ls, dtype=jnp.int32), bs)])
    tabs = _rope_tables(pos, HEAD_DIM, 1) + _rope_tables(pos, C_ROPE, 4)

    caches = (cache_moba_k.reshape(depth * n_pool, PAGE_SIZE, HEAD_DIM),
              cache_moba_v.reshape(depth * n_pool, PAGE_SIZE, HEAD_DIM),
              cache_mla_ckv.reshape(depth * n_pool, PAGE_SIZE, C_KV_RANK),
              cache_mla_krope.reshape(depth * n_pool, PAGE_SIZE, C_ROPE))

    x = jnp.concatenate([x_prompt.reshape(n_p, d), x_sample.reshape(n_s, d)], axis=0)
    xb = x.astype(BF16)
    rec = []
    for layer in range(depth):
        h = _matmul(xb, w_in_p, layer, tm, H_PAD // 4)
        q_rot, k_rot, ckv_n, kr_rot, qcat, kcat = _prep(h, tabs, g_q3, g_kv3, w_uq_p, w_ukt, layer, tm)
        vb = h[:, OFF_VB:OFF_VB + HEAD_DIM]

        ya_p, s_p, conv_p = _delta_prompt(h, w_conv, alog_vec, dtb_vec, g_norm3, layer, bp, lp, tile_a)
        yb_p = _moba_prompt(q_rot, k_rot, vb, bp, lp)
        yc_p = _mla_prompt(qcat, kcat, w_uv_p, layer, bp, lp, MOBA_BLOCK)

        h3 = h.reshape(n // ls, ls, H_PAD)
        ya_s, s_s, conv_s = _delta_sample(h3, state_conv, state_delta, w_conv, alog_vec, dtb_vec, g_norm3,
                                          layer, n_p // ls, bs, ls)
        samp = lambda a: a[n_p:].reshape(bs, ls, a.shape[-1])
        yb_s, yc_s = _attn_sample(page_table, samp(q_rot), samp(k_rot), samp(vb), samp(qcat), samp(kcat),
                                  w_uv_p, caches, layer, n_pool, bs, ls)

        mix = jnp.concatenate(
            [jnp.concatenate([ya_p, yb_p, yc_p], axis=1),
             jnp.concatenate([ya_s.reshape(n_s, A_WIDTH), yb_s.reshape(n_s, B_WIDTH), yc_s.reshape(n_s, C_WIDTH)], axis=1)],
            axis=0)
        x1, x1b = _outproj_ln(mix, w_o_b, x, ln1_g3, ln1_b3, layer, alpha, min(tm, 256))
        x, xb = _ffn_ln(x1b, x1, w_ffn_in_b, w_ffn_out_b, ln2_g3, ln2_b3, layer, alpha, tm, 512)
        rec.append((k_rot, vb, ckv_n, kr_rot, s_p, conv_p, s_s, conv_s))

    def stack(idx, lo, hi, shape):
        return jnp.stack([r[idx][lo:hi].reshape(shape) for r in rec])

    return (x[:n_p].reshape(bp, lp, d), x[n_p:].reshape(bs, ls, d),
            stack(0, 0, n_p, (bp, lp, 1, HEAD_DIM)), stack(1, 0, n_p, (bp, lp, 1, HEAD_DIM)),
            stack(2, 0, n_p, (bp, lp, C_KV_RANK)), stack(3, 0, n_p, (bp, lp, C_ROPE)),
            jnp.stack([r[4] for r in rec]), jnp.stack([r[5] for r in rec]),
            stack(0, n_p, n, (bs, ls, 1, HEAD_DIM)), stack(1, n_p, n, (bs, ls, 1, HEAD_DIM)),
            stack(2, n_p, n, (bs, ls, C_KV_RANK)), stack(3, n_p, n, (bs, ls, C_ROPE)),
            jnp.stack([r[6] for r in rec]), jnp.stack([r[7] for r in rec]))
```

```python
import functools
import math

import jax
import jax.numpy as jnp
from jax import lax
from jax.experimental import pallas as pl
from jax.experimental.pallas import tpu as pltpu

F32 = jnp.float32
BF16 = jnp.bfloat16
HIGHEST = lax.Precision.HIGHEST

D_MODEL = 2048
PAGE_SIZE = 128
HEAD_DIM = 128
A_HEADS = 8
A_WIDTH = A_HEADS * HEAD_DIM
CONV_W = 4
DELTA_CHUNK = 64
B_HEADS = 4
B_WIDTH = B_HEADS * HEAD_DIM
MOBA_BLOCK = 256
MOBA_TOPK = 3
C_HEADS = 4
C_NOPE = 128
C_ROPE = 64
C_V = 128
C_Q_RANK = 384
C_KV_RANK = 128
C_WIDTH = C_HEADS * C_V
MIX_WIDTH = A_WIDTH + B_WIDTH + C_WIDTH
ROPE_THETA = 10000.0
LN_EPS = 1e-5
RMS_EPS = 1e-6
L2_EPS = 1e-6

OFF_QKV = 0
OFF_Z = 3 * A_WIDTH
OFF_QB = 4 * A_WIDTH
OFF_CQ = OFF_QB + B_WIDTH
OFF_KB = OFF_CQ + C_Q_RANK
OFF_VB = OFF_KB + HEAD_DIM
OFF_CKV = OFF_VB + HEAD_DIM
OFF_KR = OFF_CKV + C_KV_RANK
H_PAD = 5632
LANE_BETA = C_ROPE
LANE_DECAY = C_ROPE + A_HEADS

NEG = -1e30
VMEM_LIMIT = 56 * 1024 * 1024
DELTA_SUB = 128


def _cparams(sem):
    return pltpu.CompilerParams(dimension_semantics=sem, vmem_limit_bytes=VMEM_LIMIT)


def _dot(a, b):
    return jnp.dot(a, b, preferred_element_type=F32)


def _dot_exact(a, b):
    return jnp.dot(a, b, preferred_element_type=F32, precision=HIGHEST)


def _dot_nt(a, b, precision=None):
    return lax.dot_general(a, b, (((1,), (1,)), ((), ())), preferred_element_type=F32, precision=precision)


def _dot_tn(a, b, precision=None):
    return lax.dot_general(a, b, (((0,), (0,)), ((), ())), preferred_element_type=F32, precision=precision)


def _dot_b(a, b):
    return _dot(a.astype(BF16), b.astype(BF16))


def _matmul_kernel(x_ref, w_ref, o_ref):
    o_ref[...] = _dot(x_ref[...], w_ref[...])


def _matmul(xb, w, layer, tm, tn):
    m, k = xb.shape
    n = w.shape[-1]
    return pl.pallas_call(
        _matmul_kernel,
        grid=(m // tm, n // tn),
        in_specs=[pl.BlockSpec((tm, k), lambda i, j: (i, 0)),
                  pl.BlockSpec((None, k, tn), lambda i, j: (layer, 0, j))],
        out_specs=pl.BlockSpec((tm, tn), lambda i, j: (i, j)),
        out_shape=jax.ShapeDtypeStruct((m, n), F32),
        compiler_params=_cparams(("parallel", "arbitrary")),
        name="proj_in",
    )(xb, w)


def _layer_norm_rows(y, g, b):
    mu = jnp.mean(y, axis=-1, keepdims=True)
    yc = y - mu
    var = jnp.mean(yc * yc, axis=-1, keepdims=True)
    return yc * lax.rsqrt(var + LN_EPS) * g + b


def _outproj_ln_kernel(mix_ref, w_ref, x_ref, g_ref, b_ref, o_ref, ob_ref, *, alpha):
    y = alpha * x_ref[...] + _dot(mix_ref[...], w_ref[...])
    out = _layer_norm_rows(y, g_ref[...], b_ref[...])
    o_ref[...] = out
    ob_ref[...] = out.astype(BF16)


def _outproj_ln(mix, w_o, x, g, b, layer, alpha, tm):
    m, d = x.shape
    kdim = mix.shape[1]
    return pl.pallas_call(
        functools.partial(_outproj_ln_kernel, alpha=alpha),
        grid=(m // tm,),
        in_specs=[pl.BlockSpec((tm, kdim), lambda i: (i, 0)),
                  pl.BlockSpec((None, kdim, d), lambda i: (layer, 0, 0)),
                  pl.BlockSpec((tm, d), lambda i: (i, 0)),
                  pl.BlockSpec((None, 1, d), lambda i: (layer, 0, 0)),
                  pl.BlockSpec((None, 1, d), lambda i: (layer, 0, 0))],
        out_specs=[pl.BlockSpec((tm, d), lambda i: (i, 0)),
                   pl.BlockSpec((tm, d), lambda i: (i, 0))],
        out_shape=[jax.ShapeDtypeStruct((m, d), F32), jax.ShapeDtypeStruct((m, d), BF16)],
        compiler_params=_cparams(("parallel",)),
        name="outproj_ln",
    )(mix, w_o, x, g, b)


def _ffn_ln_kernel(xb_ref, wg_ref, wu_ref, wo_ref, x_ref, g_ref, b_ref, o_ref, ob_ref, acc_ref, *, alpha):
    f = pl.program_id(1)

    @pl.when(f == 0)
    def _():
        acc_ref[...] = jnp.zeros_like(acc_ref)

    xb = xb_ref[...]
    gate = _dot(xb, wg_ref[...])
    up = _dot(xb, wu_ref[...])
    act = (jax.nn.silu(gate) * up).astype(BF16)
    acc_ref[...] += _dot(act, wo_ref[...])

    @pl.when(f == pl.num_programs(1) - 1)
    def _():
        y = alpha * x_ref[...] + acc_ref[...]
        out = _layer_norm_rows(y, g_ref[...], b_ref[...])
        o_ref[...] = out
        ob_ref[...] = out.astype(BF16)


def _ffn_ln(xb, x, w_in, w_out, g, b, layer, alpha, tm, tf):
    m, d = x.shape
    d_ff = w_out.shape[1]
    nf = d_ff // tf
    return pl.pallas_call(
        functools.partial(_ffn_ln_kernel, alpha=alpha),
        grid=(m // tm, nf),
        in_specs=[pl.BlockSpec((tm, d), lambda i, f: (i, 0)),
                  pl.BlockSpec((None, d, tf), lambda i, f: (layer, 0, f)),
                  pl.BlockSpec((None, d, tf), lambda i, f: (layer, 0, nf + f)),
                  pl.BlockSpec((None, tf, d), lambda i, f: (layer, f, 0)),
                  pl.BlockSpec((tm, d), lambda i, f: (i, 0)),
                  pl.BlockSpec((None, 1, d), lambda i, f: (layer, 0, 0)),
                  pl.BlockSpec((None, 1, d), lambda i, f: (layer, 0, 0))],
        out_specs=[pl.BlockSpec((tm, d), lambda i, f: (i, 0)),
                   pl.BlockSpec((tm, d), lambda i, f: (i, 0))],
        out_shape=[jax.ShapeDtypeStruct((m, d), F32), jax.ShapeDtypeStruct((m, d), BF16)],
        scratch_shapes=[pltpu.VMEM((tm, d), F32)],
        compiler_params=_cparams(("parallel", "arbitrary")),
        name="ffn_ln",
    )(xb, w_in, w_in, w_out, x, g, b)


def _swap_half64(a):
    lane = lax.broadcasted_iota(jnp.int32, a.shape, 1)
    first = jnp.bitwise_and(lane, 63) < 32
    return jnp.where(first, pltpu.roll(a, 96, 1), pltpu.roll(a, 32, 1))


def _prep_kernel(qb_ref, cq_ref, kb_ref, ckv_ref, kr_ref, cos_ref, sin_ref, cos64_ref, sin64_ref,
                 gq_ref, gkv_ref, wuq_ref, wukt_ref,
                 qrot_ref, krot_ref, ckvn_ref, krr_ref, qcat_ref, kcat_ref):
    cos = cos_ref[...]
    sin = sin_ref[...]
    q = qb_ref[...]
    for h in range(B_HEADS):
        xs = q[:, h * HEAD_DIM:(h + 1) * HEAD_DIM]
        qrot_ref[:, h * HEAD_DIM:(h + 1) * HEAD_DIM] = xs * cos + pltpu.roll(xs, HEAD_DIM // 2, 1) * sin
    k = kb_ref[...]
    krot_ref[...] = k * cos + pltpu.roll(k, HEAD_DIM // 2, 1) * sin

    cq = cq_ref[...]
    cqn = cq * lax.rsqrt(jnp.mean(cq * cq, axis=-1, keepdims=True) + RMS_EPS) * gq_ref[...]
    qfull = _dot(cqn.astype(BF16), wuq_ref[...])
    c64 = cos64_ref[...]
    s64 = sin64_ref[...]
    nope_w = C_HEADS * C_NOPE
    halves = []
    for half in range(2):
        a = qfull[:, nope_w + half * 128: nope_w + (half + 1) * 128]
        halves.append(a * c64[:, half * 128:(half + 1) * 128] + _swap_half64(a) * s64[:, half * 128:(half + 1) * 128])
    scale = (C_NOPE + C_ROPE) ** -0.5
    zpad = jnp.zeros((q.shape[0], 64), F32)
    for h in range(C_HEADS):
        ql = _dot(qfull[:, h * C_NOPE:(h + 1) * C_NOPE].astype(BF16), wukt_ref[h])
        hr = halves[h // 2][:, (h % 2) * C_ROPE:(h % 2 + 1) * C_ROPE]
        qcat_ref[:, h * 256:(h + 1) * 256] = (jnp.concatenate([ql, hr, zpad], axis=1) * scale).astype(BF16)

    ck = ckv_ref[...]
    ckn = ck * lax.rsqrt(jnp.mean(ck * ck, axis=-1, keepdims=True) + RMS_EPS) * gkv_ref[...]
    ckvn_ref[...] = ckn
    krb = kr_ref[...]
    krr = (krb * c64[:, :128] + _swap_half64(krb) * s64[:, :128])[:, :C_ROPE]
    krr_ref[...] = krr
    kcat_ref[...] = jnp.concatenate([ckn, krr, zpad], axis=1).astype(BF16)


def _prep(h, tabs, g_q, g_kv, w_uq, w_ukt, layer, tm):
    n = h.shape[0]
    cos128, sin128, cos64, sin64 = tabs
    row = lambda w, off: pl.BlockSpec((tm, w), lambda i: (i, off // w))
    tab = lambda w: pl.BlockSpec((tm, w), lambda i: (i, 0))
    return pl.pallas_call(
        _prep_kernel,
        grid=(n // tm,),
        in_specs=[row(B_WIDTH, OFF_QB), row(C_Q_RANK, OFF_CQ), row(HEAD_DIM, OFF_KB),
                  row(C_KV_RANK, OFF_CKV), row(128, OFF_KR),
                  tab(128), tab(128), tab(256), tab(256),
                  pl.BlockSpec((None, 1, C_Q_RANK), lambda i: (layer, 0, 0)),
                  pl.BlockSpec((None, 1, C_KV_RANK), lambda i: (layer, 0, 0)),
                  pl.BlockSpec((None, C_Q_RANK, 768), lambda i: (layer, 0, 0)),
                  pl.BlockSpec((None, C_HEADS, C_NOPE, C_KV_RANK), lambda i: (layer, 0, 0, 0))],
        out_specs=[tab(B_WIDTH), tab(HEAD_DIM), tab(C_KV_RANK), tab(C_ROPE), tab(1024), tab(256)],
        out_shape=[jax.ShapeDtypeStruct((n, B_WIDTH), F32),
                   jax.ShapeDtypeStruct((n, HEAD_DIM), F32),
                   jax.ShapeDtypeStruct((n, C_KV_RANK), F32),
                   jax.ShapeDtypeStruct((n, C_ROPE), F32),
                   jax.ShapeDtypeStruct((n, 1024), BF16),
                   jax.ShapeDtypeStruct((n, 256), BF16)],
        compiler_params=_cparams(("parallel",)),
        name="prep",
    )(h, h, h, h, h, cos128, sin128, cos64, sin64, g_q, g_kv, w_uq, w_ukt)


def _inv_unit_lower(mats, chunk):
    n = mats[0].shape[0]
    r = lax.broadcasted_iota(jnp.int32, (n, n), 0)
    col = lax.broadcasted_iota(jnp.int32, (n, n), 1)
    eye = (r == col).astype(F32)
    blk8 = jnp.right_shift(r, 3) == jnp.right_shift(col, 3)
    a8f = [jnp.where(blk8, a, 0.0) for a in mats]
    a8 = [a.astype(BF16) for a in a8f]
    xs = [eye - a for a in a8f]
    ps = [_dot(a, a) for a in a8]
    xs = [x + _dot_b(x, p) for x, p in zip(xs, ps)]
    ps = [_dot_b(p, p) for p in ps]
    xs = [x + _dot_b(x, p) for x, p in zip(xs, ps)]
    k = 8
    while k < chunk:
        sh = k.bit_length() - 1
        same2k = jnp.right_shift(r, sh + 1) == jnp.right_shift(col, sh + 1)
        samek = jnp.right_shift(r, sh) == jnp.right_shift(col, sh)
        off = same2k & jnp.logical_not(samek)
        ms = [jnp.where(off, a, 0.0).astype(BF16) for a in mats]
        xb = [x.astype(BF16) for x in xs]
        ts = [_dot(x, m) for x, m in zip(xb, ms)]
        xs = [x - _dot(t.astype(BF16), xh) for x, t, xh in zip(xs, ts, xb)]
        k *= 2
    return xs


def _wy_prepare(systems, chunk):
    n = systems[0][0].shape[0]
    r = lax.broadcasted_iota(jnp.int32, (n, n), 0)
    col = lax.broadcasted_iota(jnp.int32, (n, n), 1)
    sh = chunk.bit_length() - 1
    incl = (jnp.right_shift(r, sh) == jnp.right_shift(col, sh)) & (r >= col)
    strict = r > col
    kbs = [s[1].astype(BF16) for s in systems]
    kks = [_dot_nt(kb, kb) for kb in kbs]
    qks = [_dot_nt(s[0].astype(BF16), kb) for s, kb in zip(systems, kbs)]
    decays = [jnp.exp(jnp.where(incl, s[3] - s[4], -jnp.inf)) for s in systems]
    mats = [jnp.where(strict, s[5] * kk * dc, 0.0) for s, kk, dc in zip(systems, kks, decays)]
    tinvs = _inv_unit_lower(mats, chunk)
    egs = [jnp.exp(s[3]) for s in systems]
    rhs = [jnp.concatenate([s[5] * s[2], (s[5] * eg) * s[1]], axis=1) for s, eg in zip(systems, egs)]
    sols = [_dot_b(t, x) for t, x in zip(tinvs, rhs)]
    return [(sol[:, :HEAD_DIM], sol[:, HEAD_DIM:], s[0] * eg, qk * dc)
            for sol, s, eg, qk, dc in zip(sols, systems, egs, qks, decays)]


def _l2norm_rows(x):
    return x * lax.rsqrt(jnp.sum(x * x, axis=-1, keepdims=True) + L2_EPS)


def _decay_and_beta(bb, alog_ref, dtb_ref, valid):
    lane = lax.broadcasted_iota(jnp.int32, bb.shape, 1)
    is_decay = (lane >= LANE_DECAY) & (lane < LANE_DECAY + A_HEADS)
    g = -jnp.exp(alog_ref[...]) * jax.nn.softplus(bb + dtb_ref[...])
    g = jnp.where(is_decay & valid, g, 0.0)
    beta = jnp.where(valid, jax.nn.sigmoid(bb), 0.0)
    return g, beta


def _gated_out(o, z, gnorm):
    on = o * lax.rsqrt(jnp.mean(o * o, axis=-1, keepdims=True) + RMS_EPS) * gnorm
    return (on * jax.nn.silu(z)).astype(BF16)


def _delta_prompt_kernel(u_ref, z_ref, ba_ref, wconv_ref, alog_ref, dtb_ref, gnorm_ref, mix_hbm,
                         ya_ref, sfin_ref, cfin_ref,
                         ubuf, s_ref, qn_ref, kn_ref, vn_ref, u_s, wq_s, qkd_s, o_s, d_s, *, tile, chunk):
    del mix_hbm
    t = pl.program_id(1)
    nchunk = tile // chunk
    nsub = tile // DELTA_SUB

    @pl.when(t == 0)
    def _():
        ubuf[0:8, :] = jnp.zeros((8, ubuf.shape[1]), F32)
        s_ref[...] = jnp.zeros_like(s_ref)

    ubuf[8:8 + tile, :] = u_ref[...]
    for grp in range(3 * A_HEADS):
        lo = grp * HEAD_DIM
        y = ubuf[5:5 + tile, lo:lo + HEAD_DIM] * wconv_ref[0:1, lo:lo + HEAD_DIM]
        for i in range(1, CONV_W):
            y = y + ubuf[5 + i:5 + i + tile, lo:lo + HEAD_DIM] * wconv_ref[i:i + 1, lo:lo + HEAD_DIM]
        y = jax.nn.silu(y)
        hh = grp % A_HEADS
        if grp < A_HEADS:
            qn_ref[hh] = _l2norm_rows(y) * (HEAD_DIM ** -0.5)
        elif grp < 2 * A_HEADS:
            kn_ref[hh] = _l2norm_rows(y)
        else:
            vn_ref[hh] = y
    cfin_ref[...] = ubuf[tile + 5:tile + 8, :]
    ubuf[0:8, :] = ubuf[tile:tile + 8, :]

    g, beta = _decay_and_beta(ba_ref[...], alog_ref, dtb_ref, True)
    r = lax.broadcasted_iota(jnp.int32, (tile, tile), 0)
    col = lax.broadcasted_iota(jnp.int32, (tile, tile), 1)
    sh = chunk.bit_length() - 1
    same = jnp.right_shift(r, sh) == jnp.right_shift(col, sh)
    g_cum = _dot_exact(jnp.where(same & (r >= col), 1.0, 0.0), g)
    gt_cum = _dot_exact(g.T, jnp.where(same & (r <= col), 1.0, 0.0))

    where = [(hh, sb * DELTA_SUB) for hh in range(A_HEADS) for sb in range(nsub)]
    systems = []
    for hh, lo in where:
        hi = lo + DELTA_SUB
        ld = LANE_DECAY + hh
        lb = LANE_BETA + hh
        systems.append((qn_ref[hh, lo:hi, :], kn_ref[hh, lo:hi, :], vn_ref[hh, lo:hi, :],
                        g_cum[lo:hi, ld:ld + 1], gt_cum[ld:ld + 1, lo:hi], beta[lo:hi, lb:lb + 1]))
    for (hh, lo), (u, w, qe, qkd) in zip(where, _wy_prepare(systems, chunk)):
        u_s[hh, lo:lo + DELTA_SUB, :] = u
        for c in range(DELTA_SUB // chunk):
            a0 = lo + c * chunk
            wq_s[hh, 2 * a0:2 * a0 + chunk, :] = w[c * chunk:(c + 1) * chunk, :].astype(BF16)
            wq_s[hh, 2 * a0 + chunk:2 * a0 + 2 * chunk, :] = qe[c * chunk:(c + 1) * chunk, :].astype(BF16)
        qkd_s[hh, lo:lo + DELTA_SUB, :] = qkd.astype(BF16)

    for c in range(nchunk):
        lo, hi = c * chunk, (c + 1) * chunk
        for hh in range(A_HEADS):
            ld = LANE_DECAY + hh
            s = s_ref[hh]
            ws = _dot(wq_s[hh, 2 * lo:2 * hi, :], s.astype(BF16))
            delta = (u_s[hh, lo:hi, :] - ws[:chunk, :]).astype(BF16)
            o_s[hh, lo:hi, :] = ws[chunk:, :]
            d_s[hh, lo:hi, :] = delta
            g_last = g_cum[hi - 1:hi, ld:ld + 1]
            kd = kn_ref[hh, lo:hi, :] * jnp.exp(g_last - g_cum[lo:hi, ld:ld + 1])
            s_ref[hh] = s * jnp.exp(g_last) + _dot_tn(kd.astype(BF16), delta)

    gnorm = gnorm_ref[...]
    for hh in range(A_HEADS):
        for sb in range(nsub):
            lo, hi = sb * DELTA_SUB, (sb + 1) * DELTA_SUB
            o = o_s[hh, lo:hi, :] + _dot(qkd_s[hh, lo:hi, :], d_s[hh, lo:hi, :])
            ya_ref[lo:hi, hh * HEAD_DIM:(hh + 1) * HEAD_DIM] = _gated_out(
                o, z_ref[lo:hi, hh * HEAD_DIM:(hh + 1) * HEAD_DIM], gnorm)

    @pl.when(t == pl.num_programs(1) - 1)
    def _():
        sfin_ref[...] = s_ref[...]


def _delta_prompt(h, mix, w_conv, alog_vec, dtb_vec, g_norm, layer, bp, lp, tile):
    nt = lp // tile
    qkv_w = 3 * A_WIDTH
    kern = functools.partial(_delta_prompt_kernel, tile=tile, chunk=DELTA_CHUNK)
    head_buf = lambda rows, dt: pltpu.VMEM((A_HEADS, rows, HEAD_DIM), dt)
    return pl.pallas_call(
        kern,
        grid=(bp, nt),
        in_specs=[pl.BlockSpec((tile, qkv_w), lambda b, t: (b * nt + t, 0)),
                  pl.BlockSpec((tile, A_WIDTH), lambda b, t: (b * nt + t, OFF_Z // A_WIDTH)),
                  pl.BlockSpec((tile, 128), lambda b, t: (b * nt + t, OFF_KR // 128)),
                  pl.BlockSpec((None, CONV_W, qkv_w), lambda b, t: (layer, 0, 0)),
                  pl.BlockSpec((None, 1, 128), lambda b, t: (layer, 0, 0)),
                  pl.BlockSpec((None, 1, 128), lambda b, t: (layer, 0, 0)),
                  pl.BlockSpec((None, 1, HEAD_DIM), lambda b, t: (layer, 0, 0)),
                  pl.BlockSpec(memory_space=pl.ANY)],
        out_specs=[pl.BlockSpec((tile, A_WIDTH), lambda b, t: (b * nt + t, 0)),
                   pl.BlockSpec((None, A_HEADS, HEAD_DIM, HEAD_DIM), lambda b, t: (b, 0, 0, 0)),
                   pl.BlockSpec((None, CONV_W - 1, qkv_w), lambda b, t: (b, 0, 0))],
        out_shape=[jax.ShapeDtypeStruct(mix.shape, BF16),
                   jax.ShapeDtypeStruct((bp, A_HEADS, HEAD_DIM, HEAD_DIM), F32),
                   jax.ShapeDtypeStruct((bp, CONV_W - 1, qkv_w), F32)],
        scratch_shapes=[pltpu.VMEM((tile + 8, qkv_w), F32),
                        head_buf(HEAD_DIM, F32),
                        head_buf(tile, F32), head_buf(tile, F32), head_buf(tile, F32),
                        head_buf(tile, F32), head_buf(2 * tile, BF16), head_buf(tile, BF16),
                        head_buf(tile, F32), head_buf(tile, BF16)],
        input_output_aliases={7: 0},
        compiler_params=_cparams(("parallel", "arbitrary")),
        name="delta_prompt",
    )(h, h, h, w_conv, alog_vec, dtb_vec, g_norm, mix)


def _delta_sample_kernel(u_ref, z_ref, ba_ref, cs_ref, s0_ref, wconv_ref, alog_ref, dtb_ref, gnorm_ref,
                         ya_ref, snew_ref, cnew_ref, buf, *, ls, group):
    rows = 8
    width = buf.shape[2]
    heads = range(A_HEADS)
    rid = lax.broadcasted_iota(jnp.int32, (rows, 128), 0)
    r = lax.broadcasted_iota(jnp.int32, (rows, rows), 0)
    col = lax.broadcasted_iota(jnp.int32, (rows, rows), 1)
    tril = jnp.where(r >= col, 1.0, 0.0)

    def conv_group(gi, grp):
        lo = grp * HEAD_DIM
        y = buf[gi, 0:rows, lo:lo + HEAD_DIM] * wconv_ref[0:1, lo:lo + HEAD_DIM]
        for i in range(1, CONV_W):
            y = y + buf[gi, i:i + rows, lo:lo + HEAD_DIM] * wconv_ref[i:i + 1, lo:lo + HEAD_DIM]
        return jax.nn.silu(y)

    systems = []
    for gi in range(group):
        buf[gi, 0:CONV_W - 1, :] = cs_ref[gi]
        buf[gi, CONV_W - 1:CONV_W - 1 + ls, :] = u_ref[gi]
        buf[gi, CONV_W - 1 + ls:, :] = jnp.zeros((buf.shape[1] - (CONV_W - 1 + ls), width), F32)
        cnew_ref[gi] = buf[gi, ls:ls + CONV_W - 1, :]
        bb = jnp.concatenate([ba_ref[gi], jnp.zeros((rows - ls, 128), F32)], axis=0)
        g, beta = _decay_and_beta(bb, alog_ref, dtb_ref, rid < ls)
        g_cum = _dot_exact(tril, g)
        q = jnp.concatenate([_l2norm_rows(conv_group(gi, hh)) * (HEAD_DIM ** -0.5) for hh in heads], axis=0)
        k = jnp.concatenate([_l2norm_rows(conv_group(gi, A_HEADS + hh)) for hh in heads], axis=0)
        v = jnp.concatenate([conv_group(gi, 2 * A_HEADS + hh) for hh in heads], axis=0)
        g_col = jnp.concatenate([g_cum[:, LANE_DECAY + hh:LANE_DECAY + hh + 1] for hh in heads], axis=0)
        beta_col = jnp.concatenate([beta[:, LANE_BETA + hh:LANE_BETA + hh + 1] for hh in heads], axis=0)
        g_row = jnp.broadcast_to(g_col, (A_HEADS * rows, 128)).T[0:1, :]
        systems.append((q, k, v, g_col, g_row, beta_col))
    prepared = _wy_prepare(systems, rows)

    gnorm = gnorm_ref[...]
    for gi in range(group):
        u, w, qe, qkd = prepared[gi]
        k, g_col = systems[gi][1], systems[gi][3]
        deltas, inter = [], []
        for hh in heads:
            lo, hi = hh * rows, (hh + 1) * rows
            s = s0_ref[gi, hh]
            ws = _dot(jnp.concatenate([w[lo:hi, :], qe[lo:hi, :]], axis=0).astype(BF16), s.astype(BF16))
            delta = (u[lo:hi, :] - ws[:rows, :]).astype(BF16)
            deltas.append(delta)
            inter.append(ws[rows:, :])
            g_last = g_col[hi - 1:hi, :]
            kd = k[lo:hi, :] * jnp.exp(g_last - g_col[lo:hi, :])
            snew_ref[gi, hh] = s * jnp.exp(g_last) + _dot_tn(kd.astype(BF16), delta)
        o = jnp.concatenate(inter, axis=0) + _dot(qkd.astype(BF16), jnp.concatenate(deltas, axis=0))
        for hh in heads:
            lo = hh * HEAD_DIM
            ya_ref[gi, :, lo:lo + HEAD_DIM] = _gated_out(o[hh * rows:hh * rows + ls, :],
                                                          z_ref[gi, :, lo:lo + HEAD_DIM], gnorm)


def _delta_sample(h3, state_conv, state_delta, w_conv, alog_vec, dtb_vec, g_norm, layer, bs, ls):
    qkv_w = 3 * A_WIDTH
    group = 4 if bs % 4 == 0 else 1
    kern = functools.partial(_delta_sample_kernel, ls=ls, group=group)
    return pl.pallas_call(
        kern,
        grid=(bs // group,),
        in_specs=[pl.BlockSpec((group, ls, qkv_w), lambda b: (b, 0, 0)),
                  pl.BlockSpec((group, ls, A_WIDTH), lambda b: (b, 0, OFF_Z // A_WIDTH)),
                  pl.BlockSpec((group, ls, 128), lambda b: (b, 0, OFF_KR // 128)),
                  pl.BlockSpec((None, group, CONV_W - 1, qkv_w), lambda b: (layer, b, 0, 0)),
                  pl.BlockSpec((None, group, A_HEADS, HEAD_DIM, HEAD_DIM), lambda b: (layer, b, 0, 0, 0)),
                  pl.BlockSpec((None, CONV_W, qkv_w), lambda b: (layer, 0, 0)),
                  pl.BlockSpec((None, 1, 128), lambda b: (layer, 0, 0)),
                  pl.BlockSpec((None, 1, 128), lambda b: (layer, 0, 0)),
                  pl.BlockSpec((None, 1, HEAD_DIM), lambda b: (layer, 0, 0))],
        out_specs=[pl.BlockSpec((group, ls, A_WIDTH), lambda b: (b, 0, 0)),
                   pl.BlockSpec((group, A_HEADS, HEAD_DIM, HEAD_DIM), lambda b: (b, 0, 0, 0)),
                   pl.BlockSpec((group, CONV_W - 1, qkv_w), lambda b: (b, 0, 0))],
        out_shape=[jax.ShapeDtypeStruct((bs, ls, A_WIDTH), BF16),
                   jax.ShapeDtypeStruct((bs, A_HEADS, HEAD_DIM, HEAD_DIM), F32),
                   jax.ShapeDtypeStruct((bs, CONV_W - 1, qkv_w), F32)],
        scratch_shapes=[pltpu.VMEM((group, 16, qkv_w), F32)],
        compiler_params=_cparams(("parallel",)),
        name="delta_sample",
    )(h3, h3, h3, state_conv, state_delta, w_conv, alog_vec, dtb_vec, g_norm)


def _top_blocks(gate, n_valid, axis):
    blk = lax.broadcasted_iota(jnp.int32, gate.shape, axis)
    big = jnp.int32(2 ** 30)
    cand = blk < n_valid
    g = jnp.where(cand, gate, -jnp.inf)
    picks = []
    for _ in range(MOBA_TOPK):
        mx = jnp.max(g, axis=axis, keepdims=True)
        first = jnp.min(jnp.where((g == mx) & cand, blk, big), axis=axis, keepdims=True)
        picks.append(jnp.where(first == big, -1, first))
        hit = blk == first
        cand = cand & jnp.logical_not(hit)
        g = jnp.where(hit, -jnp.inf, g)
    return picks


def _stack_heads(x, heads, width):
    return jnp.concatenate([x[:, h * width:(h + 1) * width] for h in range(heads)], axis=0)


def _unstack_heads(x, heads, rows):
    return jnp.concatenate([x[h * rows:(h + 1) * rows, :] for h in range(heads)], axis=1)


def _flash_init(s, m_ref, l_ref, acc_ref, vt):
    m0 = jnp.max(s, axis=0, keepdims=True)
    p = jnp.exp(s - m0)
    m_ref[...] = m0
    l_ref[...] = jnp.sum(p, axis=0, keepdims=True)
    acc_ref[...] = _dot(vt, p.astype(BF16))


def _flash_step(s, m_ref, l_ref, acc_ref, vt):
    m_old = m_ref[...]
    m_new = jnp.maximum(m_old, jnp.max(s, axis=0, keepdims=True))
    alpha = jnp.exp(m_old - m_new)
    p = jnp.exp(s - m_new)
    l_ref[...] = alpha * l_ref[...] + jnp.sum(p, axis=0, keepdims=True)
    acc_ref[...] = alpha * acc_ref[...] + _dot(vt, p.astype(BF16))
    m_ref[...] = m_new


def _causal_t(s, tq):
    kpos = lax.broadcasted_iota(jnp.int32, s.shape, 0)
    qpos = jnp.bitwise_and(lax.broadcasted_iota(jnp.int32, s.shape, 1), tq - 1)
    return jnp.where(kpos <= qpos, s, NEG)


def _moba_prompt_kernel(q_ref, k_ref, v_ref, mix_hbm, o_ref, kmean_ref, kb_ref, vt_ref, m_ref, l_ref, acc_ref, *, nblk):
    del mix_hbm
    i = pl.program_id(1)
    tq = MOBA_BLOCK

    @pl.when(i == 0)
    def _():
        kmean_ref[...] = jnp.zeros_like(kmean_ref)
        for j in range(nblk):
            kj = k_ref[j * tq:(j + 1) * tq, :]
            kmean_ref[j:j + 1, :] = jnp.mean(kj, axis=0, keepdims=True)
            kb_ref[j] = kj.astype(BF16)
            vt_ref[j] = v_ref[j * tq:(j + 1) * tq, :].T.astype(BF16)

    qs = _stack_heads(q_ref[...], B_HEADS, HEAD_DIM)
    qb = (qs * (HEAD_DIM ** -0.5)).astype(BF16)
    gate = _dot_nt(kmean_ref[...], qs, precision=HIGHEST)
    picks = _top_blocks(gate, i, 0)

    _flash_init(_causal_t(_dot_nt(kb_ref[i], qb), tq), m_ref, l_ref, acc_ref, vt_ref[i])

    def body(j, carry):
        chosen = (picks[0] == j) | (picks[1] == j) | (picks[2] == j)
        s = jnp.where(chosen, _dot_nt(kb_ref[j], qb), NEG)
        _flash_step(s, m_ref, l_ref, acc_ref, vt_ref[j])
        return carry

    lax.fori_loop(0, i, body, 0)
    out_t = acc_ref[...] / l_ref[...]
    for h in range(B_HEADS):
        o_ref[:, h * HEAD_DIM:(h + 1) * HEAD_DIM] = out_t[:, h * tq:(h + 1) * tq].T.astype(BF16)


def _moba_prompt(q_rot, k_rot, v, mix, bp, lp):
    tq = MOBA_BLOCK
    nq = lp // tq
    nblk = lp // MOBA_BLOCK
    nblk_pad = -(-nblk // 8) * 8
    kern = functools.partial(_moba_prompt_kernel, nblk=nblk)
    return pl.pallas_call(
        kern,
        grid=(bp, nq),
        in_specs=[pl.BlockSpec((tq, B_WIDTH), lambda b, i: (b * nq + i, 0)),
                  pl.BlockSpec((lp, HEAD_DIM), lambda b, i: (b, 0)),
                  pl.BlockSpec((lp, HEAD_DIM), lambda b, i: (b, 0)),
                  pl.BlockSpec(memory_space=pl.ANY)],
        out_specs=pl.BlockSpec((tq, B_WIDTH), lambda b, i: (b * nq + i, A_WIDTH // B_WIDTH)),
        out_shape=jax.ShapeDtypeStruct(mix.shape, BF16),
        scratch_shapes=[pltpu.VMEM((nblk_pad, HEAD_DIM), F32),
                        pltpu.VMEM((nblk, tq, HEAD_DIM), BF16),
                        pltpu.VMEM((nblk, HEAD_DIM, tq), BF16),
                        pltpu.VMEM((1, B_HEADS * tq), F32),
                        pltpu.VMEM((1, B_HEADS * tq), F32),
                        pltpu.VMEM((HEAD_DIM, B_HEADS * tq), F32)],
        input_output_aliases={3: 0},
        compiler_params=_cparams(("parallel", "arbitrary")),
        name="moba_prompt",
    )(q_rot, k_rot, v, mix)


def _mla_prompt_kernel(q_ref, kc_ref, wuv_ref, mix_hbm, o_ref, ct_ref, m_ref, l_ref, acc_ref, *, tq, nblk):
    del mix_hbm
    i = pl.program_id(1)

    @pl.when(i == 0)
    def _():
        for j in range(nblk):
            ct_ref[j] = kc_ref[j * tq:(j + 1) * tq, :C_KV_RANK].astype(F32).T.astype(BF16)

    qs = _stack_heads(q_ref[...], C_HEADS, 256)
    r0 = pl.multiple_of(i * tq, tq)
    _flash_init(_causal_t(_dot_nt(kc_ref[pl.ds(r0, tq), :], qs), tq), m_ref, l_ref, acc_ref, ct_ref[i])

    def body(j, carry):
        c0 = pl.multiple_of(j * tq, tq)
        _flash_step(_dot_nt(kc_ref[pl.ds(c0, tq), :], qs), m_ref, l_ref, acc_ref, ct_ref[j])
        return carry

    lax.fori_loop(0, i, body, 0)
    o_lat_t = (acc_ref[...] / l_ref[...]).astype(BF16)
    for h in range(C_HEADS):
        o_ref[:, h * C_V:(h + 1) * C_V] = _dot_tn(o_lat_t[:, h * tq:(h + 1) * tq], wuv_ref[h]).astype(BF16)


def _mla_prompt(qcat, kcat, w_uv, mix, layer, bp, lp, tq):
    nq = lp // tq
    kern = functools.partial(_mla_prompt_kernel, tq=tq, nblk=nq)
    return pl.pallas_call(
        kern,
        grid=(bp, nq),
        in_specs=[pl.BlockSpec((tq, 1024), lambda b, i: (b * nq + i, 0)),
                  pl.BlockSpec((lp, 256), lambda b, i: (b, 0)),
                  pl.BlockSpec((None, C_HEADS, C_KV_RANK, C_V), lambda b, i: (layer, 0, 0, 0)),
                  pl.BlockSpec(memory_space=pl.ANY)],
        out_specs=pl.BlockSpec((tq, C_WIDTH), lambda b, i: (b * nq + i, (A_WIDTH + B_WIDTH) // C_WIDTH)),
        out_shape=jax.ShapeDtypeStruct(mix.shape, BF16),
        scratch_shapes=[pltpu.VMEM((nq, C_KV_RANK, tq), BF16),
                        pltpu.VMEM((1, C_HEADS * tq), F32),
                        pltpu.VMEM((1, C_HEADS * tq), F32),
                        pltpu.VMEM((C_KV_RANK, C_HEADS * tq), F32)],
        input_output_aliases={3: 0},
        compiler_params=_cparams(("parallel", "arbitrary")),
        name="mla_prompt",
    )(qcat, kcat, w_uv, mix)


def _attn_sample_kernel(pt_ref, qm_ref, knew_ref, vnew_ref, qc_ref, kcnew_ref, wuv_ref,
                        ck_hbm, cv_hbm, cc_hbm, cr_hbm,
                        yb_ref, yc_ref,
                        kbuf, vbuf, cbuf, rbuf, sems, kmean_ref, sm_ref, sc_ref,
                        *, n_pages, page_off, ls):
    b = pl.program_id(0)
    nbatch = pl.num_programs(0)
    slot = lax.rem(b, 2)
    n_past = n_pages * PAGE_SIZE
    nblk = n_past // MOBA_BLOCK
    ppb = MOBA_BLOCK // PAGE_SIZE
    rows = B_HEADS * ls

    def page_copies(bb, p, sl):
        pg = pt_ref[bb, p] + page_off
        dst = pl.ds(pl.multiple_of(p * PAGE_SIZE, PAGE_SIZE), PAGE_SIZE)
        return (pltpu.make_async_copy(ck_hbm.at[pg], kbuf.at[sl, dst, :], sems.at[0, sl]),
                pltpu.make_async_copy(cv_hbm.at[pg], vbuf.at[sl, dst, :], sems.at[1, sl]),
                pltpu.make_async_copy(cc_hbm.at[pg], cbuf.at[sl, dst, :], sems.at[2, sl]),
                pltpu.make_async_copy(cr_hbm.at[pg], rbuf.at[sl, p], sems.at[3, sl]))

    def start_batch(bb, sl):
        def body(p, carry):
            for cp in page_copies(bb, p, sl):
                cp.start()
            return carry
        lax.fori_loop(0, n_pages, body, 0)

    def wait_batch(bb, sl):
        def body(p, carry):
            for cp in page_copies(bb, p, sl):
                cp.wait()
            return carry
        lax.fori_loop(0, n_pages, body, 0)

    @pl.when(b == 0)
    def _():
        start_batch(b, slot)

    @pl.when(b + 1 < nbatch)
    def _():
        start_batch(b + 1, 1 - slot)

    wait_batch(b, slot)

    rtok = lax.rem(lax.broadcasted_iota(jnp.int32, (rows, ls), 0), ls)
    ctok = lax.broadcasted_iota(jnp.int32, (rows, ls), 1)
    causal_new = ctok <= rtok

    kmean_ref[...] = jnp.zeros_like(kmean_ref)
    for j in range(nblk):
        kmean_ref[j:j + 1, :] = jnp.mean(kbuf[slot, j * MOBA_BLOCK:(j + 1) * MOBA_BLOCK, :], axis=0, keepdims=True)
    qs = _stack_heads(qm_ref[...], B_HEADS, HEAD_DIM)
    qb = (qs * (HEAD_DIM ** -0.5)).astype(BF16)
    gate = _dot_nt(qs, kmean_ref[...], precision=HIGHEST)
    picks = _top_blocks(gate, nblk, 1)
    for j in range(nblk):
        kj = kbuf[slot, j * MOBA_BLOCK:(j + 1) * MOBA_BLOCK, :].astype(BF16)
        chosen = (picks[0] == j) | (picks[1] == j) | (picks[2] == j)
        sm_ref[:, j * MOBA_BLOCK:(j + 1) * MOBA_BLOCK] = jnp.where(chosen, _dot_nt(qb, kj), NEG)
    s_new = jnp.where(causal_new, _dot_nt(qb, knew_ref[...].astype(BF16)), NEG)
    s_all = sm_ref[...]
    m = jnp.maximum(jnp.max(s_all, axis=1, keepdims=True), jnp.max(s_new, axis=1, keepdims=True))
    p_new = jnp.exp(s_new - m)
    sm_ref[...] = jnp.exp(s_all - m)
    den = jnp.sum(sm_ref[...], axis=1, keepdims=True) + jnp.sum(p_new, axis=1, keepdims=True)
    acc = _dot(p_new.astype(BF16), vnew_ref[...].astype(BF16))
    for j in range(nblk):
        vj = vbuf[slot, j * MOBA_BLOCK:(j + 1) * MOBA_BLOCK, :].astype(BF16)
        acc = acc + _dot(sm_ref[:, j * MOBA_BLOCK:(j + 1) * MOBA_BLOCK].astype(BF16), vj)
    yb_ref[...] = _unstack_heads(acc / den, B_HEADS, ls).astype(BF16)

    qc = _stack_heads(qc_ref[...], C_HEADS, 256)
    ql = qc[:, :C_KV_RANK]
    qr = qc[:, C_KV_RANK:C_KV_RANK + C_ROPE]
    for j in range(nblk):
        cj = cbuf[slot, j * MOBA_BLOCK:(j + 1) * MOBA_BLOCK, :].astype(BF16)
        rj = jnp.concatenate([rbuf[slot, j * ppb + t] for t in range(ppb)], axis=1).astype(BF16)
        sc_ref[:, j * MOBA_BLOCK:(j + 1) * MOBA_BLOCK] = _dot_nt(ql, cj) + _dot(qr, rj)
    kcn = kcnew_ref[...]
    s_new = jnp.where(causal_new, _dot_nt(qc, kcn), NEG)
    s_all = sc_ref[...]
    m = jnp.maximum(jnp.max(s_all, axis=1, keepdims=True), jnp.max(s_new, axis=1, keepdims=True))
    p_new = jnp.exp(s_new - m)
    sc_ref[...] = jnp.exp(s_all - m)
    den = jnp.sum(sc_ref[...], axis=1, keepdims=True) + jnp.sum(p_new, axis=1, keepdims=True)
    acc = _dot(p_new.astype(BF16), kcn[:, :C_KV_RANK])
    for j in range(nblk):
        cj = cbuf[slot, j * MOBA_BLOCK:(j + 1) * MOBA_BLOCK, :].astype(BF16)
        acc = acc + _dot(sc_ref[:, j * MOBA_BLOCK:(j + 1) * MOBA_BLOCK].astype(BF16), cj)
    o_lat = (acc / den).astype(BF16)
    for h in range(C_HEADS):
        yc_ref[:, h * C_V:(h + 1) * C_V] = _dot(o_lat[h * ls:(h + 1) * ls, :], wuv_ref[h]).astype(BF16)


def _attn_sample(page_table, q_rot3, k_rot3, v3, qcat3, kcat3, w_uv, caches, layer, n_pool, bs, ls):
    n_pages = page_table.shape[1]
    n_past = n_pages * PAGE_SIZE
    nblk = n_past // MOBA_BLOCK
    nblk_pad = -(-nblk // 8) * 8
    rows = B_HEADS * ls
    kern = functools.partial(_attn_sample_kernel, n_pages=n_pages, page_off=layer * n_pool, ls=ls)
    new = lambda w: pl.BlockSpec((None, ls, w), lambda b, pt: (b, 0, 0))
    grid_spec = pltpu.PrefetchScalarGridSpec(
        num_scalar_prefetch=1,
        grid=(bs,),
        in_specs=[new(B_WIDTH), new(HEAD_DIM), new(HEAD_DIM), new(1024), new(256),
                  pl.BlockSpec((None, C_HEADS, C_KV_RANK, C_V), lambda b, pt: (layer, 0, 0, 0)),
                  pl.BlockSpec(memory_space=pl.ANY), pl.BlockSpec(memory_space=pl.ANY),
                  pl.BlockSpec(memory_space=pl.ANY), pl.BlockSpec(memory_space=pl.ANY)],
        out_specs=[new(B_WIDTH), new(C_WIDTH)],
        scratch_shapes=[pltpu.VMEM((2, n_past, HEAD_DIM), F32),
                        pltpu.VMEM((2, n_past, HEAD_DIM), F32),
                        pltpu.VMEM((2, n_past, C_KV_RANK), F32),
                        pltpu.VMEM((2, n_pages, C_ROPE, PAGE_SIZE), F32),
                        pltpu.SemaphoreType.DMA((4, 2)),
                        pltpu.VMEM((nblk_pad, HEAD_DIM), F32),
                        pltpu.VMEM((rows, n_past), F32),
                        pltpu.VMEM((rows, n_past), F32)])
    return pl.pallas_call(
        kern,
        grid_spec=grid_spec,
        out_shape=[jax.ShapeDtypeStruct((bs, ls, B_WIDTH), BF16),
                   jax.ShapeDtypeStruct((bs, ls, C_WIDTH), BF16)],
        compiler_params=_cparams(("arbitrary",)),
        name="attn_sample",
    )(page_table, q_rot3, k_rot3, v3, qcat3, kcat3, w_uv, *caches)


def _rope_tables(pos, d, reps):
    half = d // 2
    inv = ROPE_THETA ** (-jnp.arange(half, dtype=F32) * 2.0 / d)
    ang = pos.astype(F32)[:, None] * inv[None, :]
    cos = jnp.cos(ang)
    sin = jnp.sin(ang)
    return (jnp.tile(jnp.concatenate([cos, cos], axis=1), (1, reps)),
            jnp.tile(jnp.concatenate([-sin, sin], axis=1), (1, reps)))


def _lane_vec(v, lane0):
    depth, heads = v.shape
    out = jnp.zeros((depth, 1, 128), F32)
    return out.at[:, 0, lane0:lane0 + heads].set(v.astype(F32))


def kernel(x_prompt, x_sample, cache_moba_k, cache_moba_v, cache_mla_ckv, cache_mla_krope, state_delta, state_conv, page_table, w_in, w_conv, a_log, dt_bias, g_norm_a, g_q, w_uq, g_kv, w_uk, w_uv, w_o, ln1_g, ln1_b, w_ffn_in, w_ffn_out, ln2_g, ln2_b):
    bp, lp, d = x_prompt.shape
    bs, ls, _ = x_sample.shape
    depth = w_in.shape[0]
    n_pool = cache_moba_k.shape[1]
    n_pages = page_table.shape[1]
    n_past = n_pages * PAGE_SIZE
    n_p = bp * lp
    n_s = bs * ls
    n = n_p + n_s
    assert d == D_MODEL and lp % MOBA_BLOCK == 0 and n_past % MOBA_BLOCK == 0 and ls <= 8
    alpha = (2 * depth) ** 0.25

    tm = math.gcd(math.gcd(n_p, n_s), 512)
    tile_a = 256 if lp % 256 == 0 else DELTA_SUB

    qa_end = 4 * A_WIDTH
    o_ba = qa_end
    o_qb = o_ba + 2 * A_HEADS
    o_kb = o_qb + B_WIDTH
    o_vb = o_kb + HEAD_DIM
    o_cq = o_vb + HEAD_DIM
    o_ckv = o_cq + C_Q_RANK
    o_kr = o_ckv + C_KV_RANK
    o_end = o_kr + C_ROPE
    used = OFF_KR + C_ROPE + 2 * A_HEADS
    w_in_p = jnp.concatenate(
        [w_in[..., :qa_end], w_in[..., o_qb:o_kb], w_in[..., o_cq:o_ckv], w_in[..., o_kb:o_vb],
         w_in[..., o_vb:o_cq], w_in[..., o_ckv:o_kr], w_in[..., o_kr:o_end], w_in[..., o_ba:o_qb],
         jnp.zeros(w_in.shape[:2] + (H_PAD - used,), w_in.dtype)], axis=-1).astype(BF16)
    w_uq_p = jnp.concatenate([w_uq[..., :C_NOPE].reshape(depth, C_Q_RANK, C_HEADS * C_NOPE),
                              w_uq[..., C_NOPE:].reshape(depth, C_Q_RANK, C_HEADS * C_ROPE)], axis=-1).astype(BF16)
    w_ukt = jnp.transpose(w_uk, (0, 2, 3, 1)).astype(BF16)
    w_uv_p = jnp.transpose(w_uv, (0, 2, 1, 3)).astype(BF16)
    w_o_b = w_o.astype(BF16)
    w_ffn_in_b = w_ffn_in.astype(BF16)
    w_ffn_out_b = w_ffn_out.astype(BF16)
    alog_vec = _lane_vec(a_log, LANE_DECAY)
    dtb_vec = _lane_vec(dt_bias, LANE_DECAY)
    g_norm3 = g_norm_a.reshape(depth, 1, HEAD_DIM)
    g_q3 = g_q.reshape(depth, 1, C_Q_RANK)
    g_kv3 = g_kv.reshape(depth, 1, C_KV_RANK)
    ln1_g3, ln1_b3 = ln1_g.reshape(depth, 1, d), ln1_b.reshape(depth, 1, d)
    ln2_g3, ln2_b3 = ln2_g.reshape(depth, 1, d), ln2_b.reshape(depth, 1, d)

    pos = jnp.concatenate([jnp.tile(jnp.arange(lp, dtype=jnp.int32), bp),
                           jnp.tile(n_past + jnp.arange(ls, dtype=jnp.int32), bs)])
    tabs = _rope_tables(pos, HEAD_DIM, 1) + _rope_tables(pos, C_ROPE, 4)

    caches = (cache_moba_k.reshape(depth * n_pool, PAGE_SIZE, HEAD_DIM),
              cache_moba_v.reshape(depth * n_pool, PAGE_SIZE, HEAD_DIM),
              cache_mla_ckv.reshape(depth * n_pool, PAGE_SIZE, C_KV_RANK),
              jnp.swapaxes(cache_mla_krope, 2, 3).reshape(depth * n_pool, C_ROPE, PAGE_SIZE))

    x = jnp.concatenate([x_prompt.reshape(n_p, d), x_sample.reshape(n_s, d)], axis=0)
    xb = x.astype(BF16)
    mix = jnp.zeros((n, MIX_WIDTH), BF16)
    rec = []
    for layer in range(depth):
        h = _matmul(xb, w_in_p, layer, tm, H_PAD // 4)
        q_rot, k_rot, ckv_n, kr_rot, qcat, kcat = _prep(h, tabs, g_q3, g_kv3, w_uq_p, w_ukt, layer, tm)
        vb = h[:, OFF_VB:OFF_VB + HEAD_DIM]

        mix, s_p, conv_p = _delta_prompt(h, mix, w_conv, alog_vec, dtb_vec, g_norm3, layer, bp, lp, tile_a)
        mix = _moba_prompt(q_rot, k_rot, vb, mix, bp, lp)
        mix = _mla_prompt(qcat, kcat, w_uv_p, mix, layer, bp, lp, MOBA_BLOCK)

        samp = lambda a: a[n_p:].reshape(bs, ls, a.shape[-1])
        ya_s, s_s, conv_s = _delta_sample(samp(h), state_conv, state_delta, w_conv, alog_vec, dtb_vec, g_norm3,
                                          layer, bs, ls)
        yb_s, yc_s = _attn_sample(page_table, samp(q_rot), samp(k_rot), samp(vb), samp(qcat), samp(kcat),
                                  w_uv_p, caches, layer, n_pool, bs, ls)
        mix_s = jnp.concatenate([ya_s.reshape(n_s, A_WIDTH), yb_s.reshape(n_s, B_WIDTH), yc_s.reshape(n_s, C_WIDTH)],
                                axis=1)
        mix = lax.dynamic_update_slice(mix, mix_s, (n_p, 0))

        x1, x1b = _outproj_ln(mix, w_o_b, x, ln1_g3, ln1_b3, layer, alpha, min(tm, 256))
        x, xb = _ffn_ln(x1b, x1, w_ffn_in_b, w_ffn_out_b, ln2_g3, ln2_b3, layer, alpha, tm, 512)
        rec.append((k_rot, vb, ckv_n, kr_rot, s_p, conv_p, s_s, conv_s))

    def stack(idx, lo, hi, shape):
        return jnp.stack([r[idx][lo:hi].reshape(shape) for r in rec])

    return (x[:n_p].reshape(bp, lp, d), x[n_p:].reshape(bs, ls, d),
            stack(0, 0, n_p, (bp, lp, 1, HEAD_DIM)), stack(1, 0, n_p, (bp, lp, 1, HEAD_DIM)),
            stack(2, 0, n_p, (bp, lp, C_KV_RANK)), stack(3, 0, n_p, (bp, lp, C_ROPE)),
            jnp.stack([r[4] for r in rec]), jnp.stack([r[5] for r in rec]),
            stack(0, n_p, n, (bs, ls, 1, HEAD_DIM)), stack(1, n_p, n, (bs, ls, 1, HEAD_DIM)),
            stack(2, n_p, n, (bs, ls, C_KV_RANK)), stack(3, n_p, n, (bs, ls, C_ROPE)),
            jnp.stack([r[6] for r in rec]), jnp.stack([r[7] for r in rec]))
```

```python
import functools
import math

import jax
import jax.numpy as jnp
from jax import lax
from jax.experimental import pallas as pl
from jax.experimental.pallas import tpu as pltpu

F32 = jnp.float32
BF16 = jnp.bfloat16
HIGHEST = lax.Precision.HIGHEST

D_MODEL = 2048
PAGE_SIZE = 128
HEAD_DIM = 128
A_HEADS = 8
A_WIDTH = A_HEADS * HEAD_DIM
CONV_W = 4
DELTA_CHUNK = 64
B_HEADS = 4
B_WIDTH = B_HEADS * HEAD_DIM
MOBA_BLOCK = 256
MOBA_TOPK = 3
C_HEADS = 4
C_NOPE = 128
C_ROPE = 64
C_V = 128
C_Q_RANK = 384
C_KV_RANK = 128
C_WIDTH = C_HEADS * C_V
MIX_WIDTH = A_WIDTH + B_WIDTH + C_WIDTH
ROPE_THETA = 10000.0
LN_EPS = 1e-5
RMS_EPS = 1e-6
L2_EPS = 1e-6

OFF_QKV = 0
OFF_Z = 3 * A_WIDTH
OFF_QB = 4 * A_WIDTH
OFF_CQ = OFF_QB + B_WIDTH
OFF_KB = OFF_CQ + C_Q_RANK
OFF_VB = OFF_KB + HEAD_DIM
OFF_CKV = OFF_VB + HEAD_DIM
OFF_KR = OFF_CKV + C_KV_RANK
H_PAD = 5632
LANE_BETA = C_ROPE
LANE_DECAY = C_ROPE + A_HEADS

NEG = -1e30
VMEM_LIMIT = 56 * 1024 * 1024
DELTA_SUB = 128


def _cparams(sem):
    return pltpu.CompilerParams(dimension_semantics=sem, vmem_limit_bytes=VMEM_LIMIT)


def _dot(a, b):
    return jnp.dot(a, b, preferred_element_type=F32)


def _dot_exact(a, b):
    return jnp.dot(a, b, preferred_element_type=F32, precision=HIGHEST)


def _dot_nt(a, b, precision=None):
    return lax.dot_general(a, b, (((1,), (1,)), ((), ())), preferred_element_type=F32, precision=precision)


def _dot_tn(a, b, precision=None):
    return lax.dot_general(a, b, (((0,), (0,)), ((), ())), preferred_element_type=F32, precision=precision)


def _dot_b(a, b):
    return _dot(a.astype(BF16), b.astype(BF16))


def _matmul_kernel(x_ref, w_ref, o_ref):
    o_ref[...] = _dot(x_ref[...], w_ref[...])


def _matmul(xb, w, layer, tm, tn):
    m, k = xb.shape
    n = w.shape[-1]
    return pl.pallas_call(
        _matmul_kernel,
        grid=(m // tm, n // tn),
        in_specs=[pl.BlockSpec((tm, k), lambda i, j: (i, 0)),
                  pl.BlockSpec((None, k, tn), lambda i, j: (layer, 0, j))],
        out_specs=pl.BlockSpec((tm, tn), lambda i, j: (i, j)),
        out_shape=jax.ShapeDtypeStruct((m, n), F32),
        compiler_params=_cparams(("parallel", "arbitrary")),
        name="proj_in",
    )(xb, w)


def _layer_norm_rows(y, g, b):
    mu = jnp.mean(y, axis=-1, keepdims=True)
    yc = y - mu
    var = jnp.mean(yc * yc, axis=-1, keepdims=True)
    return yc * lax.rsqrt(var + LN_EPS) * g + b


def _outproj_ln_kernel(mix_ref, w_ref, x_ref, g_ref, b_ref, o_ref, ob_ref, *, alpha):
    y = alpha * x_ref[...] + _dot(mix_ref[...], w_ref[...])
    out = _layer_norm_rows(y, g_ref[...], b_ref[...])
    o_ref[...] = out
    ob_ref[...] = out.astype(BF16)


def _outproj_ln(mix, w_o, x, g, b, layer, alpha, tm):
    m, d = x.shape
    kdim = mix.shape[1]
    return pl.pallas_call(
        functools.partial(_outproj_ln_kernel, alpha=alpha),
        grid=(m // tm,),
        in_specs=[pl.BlockSpec((tm, kdim), lambda i: (i, 0)),
                  pl.BlockSpec((None, kdim, d), lambda i: (layer, 0, 0)),
                  pl.BlockSpec((tm, d), lambda i: (i, 0)),
                  pl.BlockSpec((None, 1, d), lambda i: (layer, 0, 0)),
                  pl.BlockSpec((None, 1, d), lambda i: (layer, 0, 0))],
        out_specs=[pl.BlockSpec((tm, d), lambda i: (i, 0)),
                   pl.BlockSpec((tm, d), lambda i: (i, 0))],
        out_shape=[jax.ShapeDtypeStruct((m, d), F32), jax.ShapeDtypeStruct((m, d), BF16)],
        compiler_params=_cparams(("parallel",)),
        name="outproj_ln",
    )(mix, w_o, x, g, b)


def _ffn_ln_kernel(xb_ref, wg_ref, wu_ref, wo_ref, x_ref, g_ref, b_ref, o_ref, ob_ref, acc_ref, *, alpha):
    f = pl.program_id(1)

    @pl.when(f == 0)
    def _():
        acc_ref[...] = jnp.zeros_like(acc_ref)

    xb = xb_ref[...]
    gate = _dot(xb, wg_ref[...])
    up = _dot(xb, wu_ref[...])
    act = (jax.nn.silu(gate) * up).astype(BF16)
    acc_ref[...] += _dot(act, wo_ref[...])

    @pl.when(f == pl.num_programs(1) - 1)
    def _():
        y = alpha * x_ref[...] + acc_ref[...]
        out = _layer_norm_rows(y, g_ref[...], b_ref[...])
        o_ref[...] = out
        ob_ref[...] = out.astype(BF16)


def _ffn_ln(xb, x, w_in, w_out, g, b, layer, alpha, tm, tf):
    m, d = x.shape
    d_ff = w_out.shape[1]
    nf = d_ff // tf
    return pl.pallas_call(
        functools.partial(_ffn_ln_kernel, alpha=alpha),
        grid=(m // tm, nf),
        in_specs=[pl.BlockSpec((tm, d), lambda i, f: (i, 0)),
                  pl.BlockSpec((None, d, tf), lambda i, f: (layer, 0, f)),
                  pl.BlockSpec((None, d, tf), lambda i, f: (layer, 0, nf + f)),
                  pl.BlockSpec((None, tf, d), lambda i, f: (layer, f, 0)),
                  pl.BlockSpec((tm, d), lambda i, f: (i, 0)),
                  pl.BlockSpec((None, 1, d), lambda i, f: (layer, 0, 0)),
                  pl.BlockSpec((None, 1, d), lambda i, f: (layer, 0, 0))],
        out_specs=[pl.BlockSpec((tm, d), lambda i, f: (i, 0)),
                   pl.BlockSpec((tm, d), lambda i, f: (i, 0))],
        out_shape=[jax.ShapeDtypeStruct((m, d), F32), jax.ShapeDtypeStruct((m, d), BF16)],
        scratch_shapes=[pltpu.VMEM((tm, d), F32)],
        compiler_params=_cparams(("parallel", "arbitrary")),
        name="ffn_ln",
    )(xb, w_in, w_in, w_out, x, g, b)


def _swap_half64(a):
    lane = lax.broadcasted_iota(jnp.int32, a.shape, 1)
    first = jnp.bitwise_and(lane, 63) < 32
    return jnp.where(first, pltpu.roll(a, 96, 1), pltpu.roll(a, 32, 1))


def _prep_kernel(qb_ref, cq_ref, kb_ref, ckv_ref, kr_ref, cos_ref, sin_ref, cos64_ref, sin64_ref,
                 gq_ref, gkv_ref, wuq_ref, wukt_ref,
                 qrot_ref, krot_ref, ckvn_ref, krr_ref, qcat_ref, kcat_ref):
    cos = cos_ref[...]
    sin = sin_ref[...]
    q = qb_ref[...]
    for h in range(B_HEADS):
        xs = q[:, h * HEAD_DIM:(h + 1) * HEAD_DIM]
        qrot_ref[:, h * HEAD_DIM:(h + 1) * HEAD_DIM] = xs * cos + pltpu.roll(xs, HEAD_DIM // 2, 1) * sin
    k = kb_ref[...]
    krot_ref[...] = k * cos + pltpu.roll(k, HEAD_DIM // 2, 1) * sin

    cq = cq_ref[...]
    cqn = cq * lax.rsqrt(jnp.mean(cq * cq, axis=-1, keepdims=True) + RMS_EPS) * gq_ref[...]
    qfull = _dot(cqn.astype(BF16), wuq_ref[...])
    c64 = cos64_ref[...]
    s64 = sin64_ref[...]
    nope_w = C_HEADS * C_NOPE
    halves = []
    for half in range(2):
        a = qfull[:, nope_w + half * 128: nope_w + (half + 1) * 128]
        halves.append(a * c64[:, half * 128:(half + 1) * 128] + _swap_half64(a) * s64[:, half * 128:(half + 1) * 128])
    scale = (C_NOPE + C_ROPE) ** -0.5
    zpad = jnp.zeros((q.shape[0], 64), F32)
    for h in range(C_HEADS):
        ql = _dot(qfull[:, h * C_NOPE:(h + 1) * C_NOPE].astype(BF16), wukt_ref[h])
        hr = halves[h // 2][:, (h % 2) * C_ROPE:(h % 2 + 1) * C_ROPE]
        qcat_ref[:, h * 256:(h + 1) * 256] = (jnp.concatenate([ql, hr, zpad], axis=1) * scale).astype(BF16)

    ck = ckv_ref[...]
    ckn = ck * lax.rsqrt(jnp.mean(ck * ck, axis=-1, keepdims=True) + RMS_EPS) * gkv_ref[...]
    ckvn_ref[...] = ckn
    krb = kr_ref[...]
    krr = (krb * c64[:, :128] + _swap_half64(krb) * s64[:, :128])[:, :C_ROPE]
    krr_ref[...] = krr
    kcat_ref[...] = jnp.concatenate([ckn, krr, zpad], axis=1).astype(BF16)


def _prep(h, tabs, g_q, g_kv, w_uq, w_ukt, layer, tm):
    n = h.shape[0]
    cos128, sin128, cos64, sin64 = tabs
    row = lambda w, off: pl.BlockSpec((tm, w), lambda i: (i, off // w))
    tab = lambda w: pl.BlockSpec((tm, w), lambda i: (i, 0))
    return pl.pallas_call(
        _prep_kernel,
        grid=(n // tm,),
        in_specs=[row(B_WIDTH, OFF_QB), row(C_Q_RANK, OFF_CQ), row(HEAD_DIM, OFF_KB),
                  row(C_KV_RANK, OFF_CKV), row(128, OFF_KR),
                  tab(128), tab(128), tab(256), tab(256),
                  pl.BlockSpec((None, 1, C_Q_RANK), lambda i: (layer, 0, 0)),
                  pl.BlockSpec((None, 1, C_KV_RANK), lambda i: (layer, 0, 0)),
                  pl.BlockSpec((None, C_Q_RANK, 768), lambda i: (layer, 0, 0)),
                  pl.BlockSpec((None, C_HEADS, C_NOPE, C_KV_RANK), lambda i: (layer, 0, 0, 0))],
        out_specs=[tab(B_WIDTH), tab(HEAD_DIM), tab(C_KV_RANK), tab(C_ROPE), tab(1024), tab(256)],
        out_shape=[jax.ShapeDtypeStruct((n, B_WIDTH), F32),
                   jax.ShapeDtypeStruct((n, HEAD_DIM), F32),
                   jax.ShapeDtypeStruct((n, C_KV_RANK), F32),
                   jax.ShapeDtypeStruct((n, C_ROPE), F32),
                   jax.ShapeDtypeStruct((n, 1024), BF16),
                   jax.ShapeDtypeStruct((n, 256), BF16)],
        compiler_params=_cparams(("parallel",)),
        name="prep",
    )(h, h, h, h, h, cos128, sin128, cos64, sin64, g_q, g_kv, w_uq, w_ukt)


def _inv_unit_lower(mats, chunk):
    n = mats[0].shape[0]
    r = lax.broadcasted_iota(jnp.int32, (n, n), 0)
    col = lax.broadcasted_iota(jnp.int32, (n, n), 1)
    eye = (r == col).astype(F32)
    blk8 = jnp.right_shift(r, 3) == jnp.right_shift(col, 3)
    a8f = [jnp.where(blk8, a, 0.0) for a in mats]
    a8 = [a.astype(BF16) for a in a8f]
    xs = [eye - a for a in a8f]
    ps = [_dot(a, a) for a in a8]
    xs = [x + _dot_b(x, p) for x, p in zip(xs, ps)]
    ps = [_dot_b(p, p) for p in ps]
    xs = [x + _dot_b(x, p) for x, p in zip(xs, ps)]
    k = 8
    while k < chunk:
        sh = k.bit_length() - 1
        same2k = jnp.right_shift(r, sh + 1) == jnp.right_shift(col, sh + 1)
        samek = jnp.right_shift(r, sh) == jnp.right_shift(col, sh)
        off = same2k & jnp.logical_not(samek)
        ms = [jnp.where(off, a, 0.0).astype(BF16) for a in mats]
        xb = [x.astype(BF16) for x in xs]
        ts = [_dot(x, m) for x, m in zip(xb, ms)]
        xs = [x - _dot(t.astype(BF16), xh) for x, t, xh in zip(xs, ts, xb)]
        k *= 2
    return xs


def _wy_prepare(systems, chunk):
    n = systems[0][0].shape[0]
    r = lax.broadcasted_iota(jnp.int32, (n, n), 0)
    col = lax.broadcasted_iota(jnp.int32, (n, n), 1)
    sh = chunk.bit_length() - 1
    incl = (jnp.right_shift(r, sh) == jnp.right_shift(col, sh)) & (r >= col)
    strict = r > col
    kbs = [s[1].astype(BF16) for s in systems]
    kks = [_dot_nt(kb, kb) for kb in kbs]
    qks = [_dot_nt(s[0].astype(BF16), kb) for s, kb in zip(systems, kbs)]
    decays = [jnp.exp(jnp.where(incl, s[3] - s[4], -jnp.inf)) for s in systems]
    mats = [jnp.where(strict, s[5] * kk * dc, 0.0) for s, kk, dc in zip(systems, kks, decays)]
    tinvs = _inv_unit_lower(mats, chunk)
    egs = [jnp.exp(s[3]) for s in systems]
    rhs = [jnp.concatenate([s[5] * s[2], (s[5] * eg) * s[1]], axis=1) for s, eg in zip(systems, egs)]
    sols = [_dot_b(t, x) for t, x in zip(tinvs, rhs)]
    return [(sol[:, :HEAD_DIM], sol[:, HEAD_DIM:], s[0] * eg, qk * dc)
            for sol, s, eg, qk, dc in zip(sols, systems, egs, qks, decays)]


def _l2norm_rows(x):
    return x * lax.rsqrt(jnp.sum(x * x, axis=-1, keepdims=True) + L2_EPS)


def _decay_and_beta(bb, alog_ref, dtb_ref, valid):
    lane = lax.broadcasted_iota(jnp.int32, bb.shape, 1)
    is_decay = (lane >= LANE_DECAY) & (lane < LANE_DECAY + A_HEADS)
    g = -jnp.exp(alog_ref[...]) * jax.nn.softplus(bb + dtb_ref[...])
    g = jnp.where(is_decay & valid, g, 0.0)
    beta = jnp.where(valid, jax.nn.sigmoid(bb), 0.0)
    return g, beta


def _gated_out(o, z, gnorm):
    on = o * lax.rsqrt(jnp.mean(o * o, axis=-1, keepdims=True) + RMS_EPS) * gnorm
    return (on * jax.nn.silu(z)).astype(BF16)


def _delta_prompt_kernel(u_ref, z_ref, ba_ref, wconv_ref, alog_ref, dtb_ref, gnorm_ref, mix_hbm,
                         ya_ref, sfin_ref, cfin_ref,
                         ubuf, s_ref, qn_ref, kn_ref, vn_ref, u_s, wq_s, qkd_s, o_s, d_s, *, tile, chunk):
    del mix_hbm
    t = pl.program_id(1)
    nchunk = tile // chunk
    nsub = tile // DELTA_SUB

    @pl.when(t == 0)
    def _():
        ubuf[0:8, :] = jnp.zeros((8, ubuf.shape[1]), F32)
        s_ref[...] = jnp.zeros_like(s_ref)

    ubuf[8:8 + tile, :] = u_ref[...]
    for grp in range(3 * A_HEADS):
        lo = grp * HEAD_DIM
        y = ubuf[5:5 + tile, lo:lo + HEAD_DIM] * wconv_ref[0:1, lo:lo + HEAD_DIM]
        for i in range(1, CONV_W):
            y = y + ubuf[5 + i:5 + i + tile, lo:lo + HEAD_DIM] * wconv_ref[i:i + 1, lo:lo + HEAD_DIM]
        y = jax.nn.silu(y)
        hh = grp % A_HEADS
        if grp < A_HEADS:
            qn_ref[hh] = _l2norm_rows(y) * (HEAD_DIM ** -0.5)
        elif grp < 2 * A_HEADS:
            kn_ref[hh] = _l2norm_rows(y)
        else:
            vn_ref[hh] = y
    cfin_ref[...] = ubuf[tile + 5:tile + 8, :]
    ubuf[0:8, :] = ubuf[tile:tile + 8, :]

    g, beta = _decay_and_beta(ba_ref[...], alog_ref, dtb_ref, True)
    r = lax.broadcasted_iota(jnp.int32, (tile, tile), 0)
    col = lax.broadcasted_iota(jnp.int32, (tile, tile), 1)
    sh = chunk.bit_length() - 1
    same = jnp.right_shift(r, sh) == jnp.right_shift(col, sh)
    g_cum = _dot_exact(jnp.where(same & (r >= col), 1.0, 0.0), g)
    gt_cum = _dot_exact(g.T, jnp.where(same & (r <= col), 1.0, 0.0))

    where = [(hh, sb * DELTA_SUB) for hh in range(A_HEADS) for sb in range(nsub)]
    systems = []
    for hh, lo in where:
        hi = lo + DELTA_SUB
        ld = LANE_DECAY + hh
        lb = LANE_BETA + hh
        systems.append((qn_ref[hh, lo:hi, :], kn_ref[hh, lo:hi, :], vn_ref[hh, lo:hi, :],
                        g_cum[lo:hi, ld:ld + 1], gt_cum[ld:ld + 1, lo:hi], beta[lo:hi, lb:lb + 1]))
    for (hh, lo), (u, w, qe, qkd) in zip(where, _wy_prepare(systems, chunk)):
        u_s[hh, lo:lo + DELTA_SUB, :] = u
        for c in range(DELTA_SUB // chunk):
            a0 = lo + c * chunk
            wq_s[hh, 2 * a0:2 * a0 + chunk, :] = w[c * chunk:(c + 1) * chunk, :].astype(BF16)
            wq_s[hh, 2 * a0 + chunk:2 * a0 + 2 * chunk, :] = qe[c * chunk:(c + 1) * chunk, :].astype(BF16)
        qkd_s[hh, lo:lo + DELTA_SUB, :] = qkd.astype(BF16)

    for c in range(nchunk):
        lo, hi = c * chunk, (c + 1) * chunk
        for hh in range(A_HEADS):
            ld = LANE_DECAY + hh
            s = s_ref[hh]
            ws = _dot(wq_s[hh, 2 * lo:2 * hi, :], s.astype(BF16))
            delta = (u_s[hh, lo:hi, :] - ws[:chunk, :]).astype(BF16)
            o_s[hh, lo:hi, :] = ws[chunk:, :]
            d_s[hh, lo:hi, :] = delta
            g_last = g_cum[hi - 1:hi, ld:ld + 1]
            kd = kn_ref[hh, lo:hi, :] * jnp.exp(g_last - g_cum[lo:hi, ld:ld + 1])
            s_ref[hh] = s * jnp.exp(g_last) + _dot_tn(kd.astype(BF16), delta)

    gnorm = gnorm_ref[...]
    for hh in range(A_HEADS):
        for sb in range(nsub):
            lo, hi = sb * DELTA_SUB, (sb + 1) * DELTA_SUB
            o = o_s[hh, lo:hi, :] + _dot(qkd_s[hh, lo:hi, :], d_s[hh, lo:hi, :])
            ya_ref[lo:hi, hh * HEAD_DIM:(hh + 1) * HEAD_DIM] = _gated_out(
                o, z_ref[lo:hi, hh * HEAD_DIM:(hh + 1) * HEAD_DIM], gnorm)

    @pl.when(t == pl.num_programs(1) - 1)
    def _():
        sfin_ref[...] = s_ref[...]


def _delta_prompt(h, mix, w_conv, alog_vec, dtb_vec, g_norm, layer, bp, lp, tile):
    nt = lp // tile
    qkv_w = 3 * A_WIDTH
    kern = functools.partial(_delta_prompt_kernel, tile=tile, chunk=DELTA_CHUNK)
    head_buf = lambda rows, dt: pltpu.VMEM((A_HEADS, rows, HEAD_DIM), dt)
    return pl.pallas_call(
        kern,
        grid=(bp, nt),
        in_specs=[pl.BlockSpec((tile, qkv_w), lambda b, t: (b * nt + t, 0)),
                  pl.BlockSpec((tile, A_WIDTH), lambda b, t: (b * nt + t, OFF_Z // A_WIDTH)),
                  pl.BlockSpec((tile, 128), lambda b, t: (b * nt + t, OFF_KR // 128)),
                  pl.BlockSpec((None, CONV_W, qkv_w), lambda b, t: (layer, 0, 0)),
                  pl.BlockSpec((None, 1, 128), lambda b, t: (layer, 0, 0)),
                  pl.BlockSpec((None, 1, 128), lambda b, t: (layer, 0, 0)),
                  pl.BlockSpec((None, 1, HEAD_DIM), lambda b, t: (layer, 0, 0)),
                  pl.BlockSpec(memory_space=pl.ANY)],
        out_specs=[pl.BlockSpec((tile, A_WIDTH), lambda b, t: (b * nt + t, 0)),
                   pl.BlockSpec((None, A_HEADS, HEAD_DIM, HEAD_DIM), lambda b, t: (b, 0, 0, 0)),
                   pl.BlockSpec((None, CONV_W - 1, qkv_w), lambda b, t: (b, 0, 0))],
        out_shape=[jax.ShapeDtypeStruct(mix.shape, BF16),
                   jax.ShapeDtypeStruct((bp, A_HEADS, HEAD_DIM, HEAD_DIM), F32),
                   jax.ShapeDtypeStruct((bp, CONV_W - 1, qkv_w), F32)],
        scratch_shapes=[pltpu.VMEM((tile + 8, qkv_w), F32),
                        head_buf(HEAD_DIM, F32),
                        head_buf(tile, F32), head_buf(tile, F32), head_buf(tile, F32),
                        head_buf(tile, F32), head_buf(2 * tile, BF16), head_buf(tile, BF16),
                        head_buf(tile, F32), head_buf(tile, BF16)],
        input_output_aliases={7: 0},
        compiler_params=_cparams(("parallel", "arbitrary")),
        name="delta_prompt",
    )(h, h, h, w_conv, alog_vec, dtb_vec, g_norm, mix)


def _delta_sample_kernel(u_ref, z_ref, ba_ref, cs_ref, s0_ref, wconv_ref, alog_ref, dtb_ref, gnorm_ref,
                         ya_ref, snew_ref, cnew_ref, buf, *, ls, group):
    rows = 8
    width = buf.shape[2]
    heads = range(A_HEADS)
    rid = lax.broadcasted_iota(jnp.int32, (rows, 128), 0)
    r = lax.broadcasted_iota(jnp.int32, (rows, rows), 0)
    col = lax.broadcasted_iota(jnp.int32, (rows, rows), 1)
    tril = jnp.where(r >= col, 1.0, 0.0)

    def conv_group(gi, grp):
        lo = grp * HEAD_DIM
        y = buf[gi, 0:rows, lo:lo + HEAD_DIM] * wconv_ref[0:1, lo:lo + HEAD_DIM]
        for i in range(1, CONV_W):
            y = y + buf[gi, i:i + rows, lo:lo + HEAD_DIM] * wconv_ref[i:i + 1, lo:lo + HEAD_DIM]
        return jax.nn.silu(y)

    systems = []
    for gi in range(group):
        buf[gi, 0:CONV_W - 1, :] = cs_ref[gi]
        buf[gi, CONV_W - 1:CONV_W - 1 + ls, :] = u_ref[gi]
        buf[gi, CONV_W - 1 + ls:, :] = jnp.zeros((buf.shape[1] - (CONV_W - 1 + ls), width), F32)
        cnew_ref[gi] = buf[gi, ls:ls + CONV_W - 1, :]
        bb = jnp.concatenate([ba_ref[gi], jnp.zeros((rows - ls, 128), F32)], axis=0)
        g, beta = _decay_and_beta(bb, alog_ref, dtb_ref, rid < ls)
        g_cum = _dot_exact(tril, g)
        q = jnp.concatenate([_l2norm_rows(conv_group(gi, hh)) * (HEAD_DIM ** -0.5) for hh in heads], axis=0)
        k = jnp.concatenate([_l2norm_rows(conv_group(gi, A_HEADS + hh)) for hh in heads], axis=0)
        v = jnp.concatenate([conv_group(gi, 2 * A_HEADS + hh) for hh in heads], axis=0)
        g_col = jnp.concatenate([g_cum[:, LANE_DECAY + hh:LANE_DECAY + hh + 1] for hh in heads], axis=0)
        beta_col = jnp.concatenate([beta[:, LANE_BETA + hh:LANE_BETA + hh + 1] for hh in heads], axis=0)
        g_row = jnp.broadcast_to(g_col, (A_HEADS * rows, 128)).T[0:1, :]
        systems.append((q, k, v, g_col, g_row, beta_col))
    prepared = _wy_prepare(systems, rows)

    nrow = A_HEADS * rows
    wide = A_HEADS * HEAD_DIM
    row_head = jnp.bitwise_and(jnp.right_shift(lax.broadcasted_iota(jnp.int32, (2 * nrow, wide), 0), 3), A_HEADS - 1)
    col_head = jnp.right_shift(lax.broadcasted_iota(jnp.int32, (2 * nrow, wide), 1), 7)
    own_head = row_head == col_head
    gnorm = gnorm_ref[...]
    for gi in range(group):
        u, w, qe, qkd = prepared[gi]
        k, g_col = systems[gi][1], systems[gi][3]
        s_stack = s0_ref[gi].reshape(wide, HEAD_DIM)
        lhs = jnp.tile(jnp.concatenate([w, qe], axis=0), (1, A_HEADS))
        ws = _dot(jnp.where(own_head, lhs, 0.0).astype(BF16), s_stack.astype(BF16))
        delta = u - ws[:nrow, :]
        o = ws[nrow:, :] + _dot_b(qkd, delta)
        g_last = jnp.concatenate([jnp.broadcast_to(g_col[(hh + 1) * rows - 1:(hh + 1) * rows, :], (rows, 1))
                                  for hh in heads], axis=0)
        kd = k * jnp.exp(g_last - g_col)
        d_wide = jnp.where(own_head[:nrow, :], jnp.tile(delta, (1, A_HEADS)), 0.0)
        upd = _dot_tn(kd.astype(BF16), d_wide.astype(BF16))
        for hh in heads:
            lo = hh * HEAD_DIM
            decay_h = jnp.exp(g_col[(hh + 1) * rows - 1:(hh + 1) * rows, :])
            snew_ref[gi, hh] = s0_ref[gi, hh] * decay_h + upd[:, lo:lo + HEAD_DIM]
            ya_ref[gi, :, lo:lo + HEAD_DIM] = _gated_out(o[hh * rows:hh * rows + ls, :],
                                                          z_ref[gi, :, lo:lo + HEAD_DIM], gnorm)


def _delta_sample(h3, state_conv, state_delta, w_conv, alog_vec, dtb_vec, g_norm, layer, bs, ls):
    qkv_w = 3 * A_WIDTH
    group = 4 if bs % 4 == 0 else 1
    kern = functools.partial(_delta_sample_kernel, ls=ls, group=group)
    return pl.pallas_call(
        kern,
        grid=(bs // group,),
        in_specs=[pl.BlockSpec((group, ls, qkv_w), lambda b: (b, 0, 0)),
                  pl.BlockSpec((group, ls, A_WIDTH), lambda b: (b, 0, OFF_Z // A_WIDTH)),
                  pl.BlockSpec((group, ls, 128), lambda b: (b, 0, OFF_KR // 128)),
                  pl.BlockSpec((None, group, CONV_W - 1, qkv_w), lambda b: (layer, b, 0, 0)),
                  pl.BlockSpec((None, group, A_HEADS, HEAD_DIM, HEAD_DIM), lambda b: (layer, b, 0, 0, 0)),
                  pl.BlockSpec((None, CONV_W, qkv_w), lambda b: (layer, 0, 0)),
                  pl.BlockSpec((None, 1, 128), lambda b: (layer, 0, 0)),
                  pl.BlockSpec((None, 1, 128), lambda b: (layer, 0, 0)),
                  pl.BlockSpec((None, 1, HEAD_DIM), lambda b: (layer, 0, 0))],
        out_specs=[pl.BlockSpec((group, ls, A_WIDTH), lambda b: (b, 0, 0)),
                   pl.BlockSpec((group, A_HEADS, HEAD_DIM, HEAD_DIM), lambda b: (b, 0, 0, 0)),
                   pl.BlockSpec((group, CONV_W - 1, qkv_w), lambda b: (b, 0, 0))],
        out_shape=[jax.ShapeDtypeStruct((bs, ls, A_WIDTH), BF16),
                   jax.ShapeDtypeStruct((bs, A_HEADS, HEAD_DIM, HEAD_DIM), F32),
                   jax.ShapeDtypeStruct((bs, CONV_W - 1, qkv_w), F32)],
        scratch_shapes=[pltpu.VMEM((group, 16, qkv_w), F32)],
        compiler_params=_cparams(("parallel",)),
        name="delta_sample",
    )(h3, h3, h3, state_conv, state_delta, w_conv, alog_vec, dtb_vec, g_norm)


def _top_blocks(gate, n_valid, axis):
    blk = lax.broadcasted_iota(jnp.int32, gate.shape, axis)
    big = jnp.int32(2 ** 30)
    cand = blk < n_valid
    g = jnp.where(cand, gate, -jnp.inf)
    picks = []
    for _ in range(MOBA_TOPK):
        mx = jnp.max(g, axis=axis, keepdims=True)
        first = jnp.min(jnp.where((g == mx) & cand, blk, big), axis=axis, keepdims=True)
        picks.append(jnp.where(first == big, -1, first))
        hit = blk == first
        cand = cand & jnp.logical_not(hit)
        g = jnp.where(hit, -jnp.inf, g)
    return picks


def _stack_heads(x, heads, width):
    return jnp.concatenate([x[:, h * width:(h + 1) * width] for h in range(heads)], axis=0)


def _unstack_heads(x, heads, rows):
    return jnp.concatenate([x[h * rows:(h + 1) * rows, :] for h in range(heads)], axis=1)


def _flash_init(s, m_ref, l_ref, acc_ref, vt):
    m0 = jnp.max(s, axis=0, keepdims=True)
    p = jnp.exp(s - m0)
    m_ref[...] = m0
    l_ref[...] = jnp.sum(p, axis=0, keepdims=True)
    acc_ref[...] = _dot(vt, p.astype(BF16))


def _flash_step(blocks, m_ref, l_ref, acc_ref):
    m_old = m_ref[...]
    m_new = m_old
    for s, _ in blocks:
        m_new = jnp.maximum(m_new, jnp.max(s, axis=0, keepdims=True))
    alpha = jnp.exp(m_old - m_new)
    l_new = alpha * l_ref[...]
    acc = alpha * acc_ref[...]
    for s, vt in blocks:
        p = jnp.exp(s - m_new)
        l_new = l_new + jnp.sum(p, axis=0, keepdims=True)
        acc = acc + _dot(vt, p.astype(BF16))
    l_ref[...] = l_new
    acc_ref[...] = acc
    m_ref[...] = m_new


def _loop_in_pairs(n, block):
    def pair(jp, carry):
        block([2 * jp, 2 * jp + 1])
        return carry

    lax.fori_loop(0, lax.div(n, 2), pair, 0)

    @pl.when(lax.rem(n, 2) == 1)
    def _():
        block([n - 1])


def _causal_t(s, tq):
    kpos = lax.broadcasted_iota(jnp.int32, s.shape, 0)
    qpos = jnp.bitwise_and(lax.broadcasted_iota(jnp.int32, s.shape, 1), tq - 1)
    return jnp.where(kpos <= qpos, s, NEG)


def _moba_prompt_kernel(q_ref, k_ref, v_ref, mix_hbm, o_ref, kmean_ref, kb_ref, vt_ref, m_ref, l_ref, acc_ref, *, nblk):
    del mix_hbm
    i = pl.program_id(1)
    tq = MOBA_BLOCK

    @pl.when(i == 0)
    def _():
        kmean_ref[...] = jnp.zeros_like(kmean_ref)
        for j in range(nblk):
            kj = k_ref[j * tq:(j + 1) * tq, :]
            kmean_ref[j:j + 1, :] = jnp.mean(kj, axis=0, keepdims=True)
            kb_ref[j] = kj.astype(BF16)
            vt_ref[j] = v_ref[j * tq:(j + 1) * tq, :].T.astype(BF16)

    qs = _stack_heads(q_ref[...], B_HEADS, HEAD_DIM)
    qb = (qs * (HEAD_DIM ** -0.5)).astype(BF16)
    gate = _dot_nt(kmean_ref[...], qs, precision=HIGHEST)
    picks = _top_blocks(gate, i, 0)

    _flash_init(_causal_t(_dot_nt(kb_ref[i], qb), tq), m_ref, l_ref, acc_ref, vt_ref[i])

    def past_blocks(ids):
        blocks = []
        for j in ids:
            chosen = (picks[0] == j) | (picks[1] == j) | (picks[2] == j)
            blocks.append((jnp.where(chosen, _dot_nt(kb_ref[j], qb), NEG), vt_ref[j]))
        _flash_step(blocks, m_ref, l_ref, acc_ref)

    _loop_in_pairs(i, past_blocks)
    out_t = acc_ref[...] / l_ref[...]
    for h in range(B_HEADS):
        o_ref[:, h * HEAD_DIM:(h + 1) * HEAD_DIM] = out_t[:, h * tq:(h + 1) * tq].T.astype(BF16)


def _moba_prompt(q_rot, k_rot, v, mix, bp, lp):
    tq = MOBA_BLOCK
    nq = lp // tq
    nblk = lp // MOBA_BLOCK
    nblk_pad = -(-nblk // 8) * 8
    kern = functools.partial(_moba_prompt_kernel, nblk=nblk)
    return pl.pallas_call(
        kern,
        grid=(bp, nq),
        in_specs=[pl.BlockSpec((tq, B_WIDTH), lambda b, i: (b * nq + i, 0)),
                  pl.BlockSpec((lp, HEAD_DIM), lambda b, i: (b, 0)),
                  pl.BlockSpec((lp, HEAD_DIM), lambda b, i: (b, 0)),
                  pl.BlockSpec(memory_space=pl.ANY)],
        out_specs=pl.BlockSpec((tq, B_WIDTH), lambda b, i: (b * nq + i, A_WIDTH // B_WIDTH)),
        out_shape=jax.ShapeDtypeStruct(mix.shape, BF16),
        scratch_shapes=[pltpu.VMEM((nblk_pad, HEAD_DIM), F32),
                        pltpu.VMEM((nblk, tq, HEAD_DIM), BF16),
                        pltpu.VMEM((nblk, HEAD_DIM, tq), BF16),
                        pltpu.VMEM((1, B_HEADS * tq), F32),
                        pltpu.VMEM((1, B_HEADS * tq), F32),
                        pltpu.VMEM((HEAD_DIM, B_HEADS * tq), F32)],
        input_output_aliases={3: 0},
        compiler_params=_cparams(("parallel", "arbitrary")),
        name="moba_prompt",
    )(q_rot, k_rot, v, mix)


def _mla_prompt_kernel(q_ref, kc_ref, wuv_ref, mix_hbm, o_ref, ct_ref, m_ref, l_ref, acc_ref, *, tq, nblk):
    del mix_hbm
    i = pl.program_id(1)

    @pl.when(i == 0)
    def _():
        for j in range(nblk):
            ct_ref[j] = kc_ref[j * tq:(j + 1) * tq, :C_KV_RANK].astype(F32).T.astype(BF16)

    qs = _stack_heads(q_ref[...], C_HEADS, 256)
    r0 = pl.multiple_of(i * tq, tq)
    _flash_init(_causal_t(_dot_nt(kc_ref[pl.ds(r0, tq), :], qs), tq), m_ref, l_ref, acc_ref, ct_ref[i])

    def past_blocks(ids):
        blocks = []
        for j in ids:
            c0 = pl.multiple_of(j * tq, tq)
            blocks.append((_dot_nt(kc_ref[pl.ds(c0, tq), :], qs), ct_ref[j]))
        _flash_step(blocks, m_ref, l_ref, acc_ref)

    _loop_in_pairs(i, past_blocks)
    o_lat_t = (acc_ref[...] / l_ref[...]).astype(BF16)
    for h in range(C_HEADS):
        o_ref[:, h * C_V:(h + 1) * C_V] = _dot_tn(o_lat_t[:, h * tq:(h + 1) * tq], wuv_ref[h]).astype(BF16)


def _mla_prompt(qcat, kcat, w_uv, mix, layer, bp, lp, tq):
    nq = lp // tq
    kern = functools.partial(_mla_prompt_kernel, tq=tq, nblk=nq)
    return pl.pallas_call(
        kern,
        grid=(bp, nq),
        in_specs=[pl.BlockSpec((tq, 1024), lambda b, i: (b * nq + i, 0)),
                  pl.BlockSpec((lp, 256), lambda b, i: (b, 0)),
                  pl.BlockSpec((None, C_HEADS, C_KV_RANK, C_V), lambda b, i: (layer, 0, 0, 0)),
                  pl.BlockSpec(memory_space=pl.ANY)],
        out_specs=pl.BlockSpec((tq, C_WIDTH), lambda b, i: (b * nq + i, (A_WIDTH + B_WIDTH) // C_WIDTH)),
        out_shape=jax.ShapeDtypeStruct(mix.shape, BF16),
        scratch_shapes=[pltpu.VMEM((nq, C_KV_RANK, tq), BF16),
                        pltpu.VMEM((1, C_HEADS * tq), F32),
                        pltpu.VMEM((1, C_HEADS * tq), F32),
                        pltpu.VMEM((C_KV_RANK, C_HEADS * tq), F32)],
        input_output_aliases={3: 0},
        compiler_params=_cparams(("parallel", "arbitrary")),
        name="mla_prompt",
    )(qcat, kcat, w_uv, mix)


def _attn_sample_kernel(pt_ref, qm_ref, knew_ref, vnew_ref, qc_ref, kcnew_ref, wuv_ref,
                        ck_hbm, cv_hbm, cc_hbm, cr_hbm,
                        yb_ref, yc_ref,
                        kbuf, vbuf, cbuf, rbuf, sems, kmean_ref, sm_ref, sc_ref,
                        *, n_pages, page_off, ls):
    b = pl.program_id(0)
    nbatch = pl.num_programs(0)
    slot = lax.rem(b, 2)
    n_past = n_pages * PAGE_SIZE
    nblk = n_past // MOBA_BLOCK
    ppb = MOBA_BLOCK // PAGE_SIZE
    rows = B_HEADS * ls

    def page_copies(bb, p, sl):
        pg = pt_ref[bb, p] + page_off
        dst = pl.ds(pl.multiple_of(p * PAGE_SIZE, PAGE_SIZE), PAGE_SIZE)
        return (pltpu.make_async_copy(ck_hbm.at[pg], kbuf.at[sl, dst, :], sems.at[0, sl]),
                pltpu.make_async_copy(cv_hbm.at[pg], vbuf.at[sl, dst, :], sems.at[1, sl]),
                pltpu.make_async_copy(cc_hbm.at[pg], cbuf.at[sl, dst, :], sems.at[2, sl]),
                pltpu.make_async_copy(cr_hbm.at[pg], rbuf.at[sl, p], sems.at[3, sl]))

    def start_batch(bb, sl):
        def body(p, carry):
            for cp in page_copies(bb, p, sl):
                cp.start()
            return carry
        lax.fori_loop(0, n_pages, body, 0)

    def wait_batch(bb, sl):
        def body(p, carry):
            for cp in page_copies(bb, p, sl):
                cp.wait()
            return carry
        lax.fori_loop(0, n_pages, body, 0)

    @pl.when(b == 0)
    def _():
        start_batch(b, slot)

    @pl.when(b + 1 < nbatch)
    def _():
        start_batch(b + 1, 1 - slot)

    wait_batch(b, slot)

    rtok = lax.rem(lax.broadcasted_iota(jnp.int32, (rows, ls), 0), ls)
    ctok = lax.broadcasted_iota(jnp.int32, (rows, ls), 1)
    causal_new = ctok <= rtok

    kmean_ref[...] = jnp.zeros_like(kmean_ref)
    for j in range(nblk):
        kmean_ref[j:j + 1, :] = jnp.mean(kbuf[slot, j * MOBA_BLOCK:(j + 1) * MOBA_BLOCK, :], axis=0, keepdims=True)
    qs = _stack_heads(qm_ref[...], B_HEADS, HEAD_DIM)
    qb = (qs * (HEAD_DIM ** -0.5)).astype(BF16)
    gate = _dot_nt(qs, kmean_ref[...], precision=HIGHEST)
    picks = _top_blocks(gate, nblk, 1)
    for j in range(nblk):
        kj = kbuf[slot, j * MOBA_BLOCK:(j + 1) * MOBA_BLOCK, :].astype(BF16)
        chosen = (picks[0] == j) | (picks[1] == j) | (picks[2] == j)
        sm_ref[:, j * MOBA_BLOCK:(j + 1) * MOBA_BLOCK] = jnp.where(chosen, _dot_nt(qb, kj), NEG)
    s_new = jnp.where(causal_new, _dot_nt(qb, knew_ref[...].astype(BF16)), NEG)
    s_all = sm_ref[...]
    m = jnp.maximum(jnp.max(s_all, axis=1, keepdims=True), jnp.max(s_new, axis=1, keepdims=True))
    p_new = jnp.exp(s_new - m)
    sm_ref[...] = jnp.exp(s_all - m)
    den = jnp.sum(sm_ref[...], axis=1, keepdims=True) + jnp.sum(p_new, axis=1, keepdims=True)
    acc = _dot(p_new.astype(BF16), vnew_ref[...].astype(BF16))
    for j in range(nblk):
        vj = vbuf[slot, j * MOBA_BLOCK:(j + 1) * MOBA_BLOCK, :].astype(BF16)
        acc = acc + _dot(sm_ref[:, j * MOBA_BLOCK:(j + 1) * MOBA_BLOCK].astype(BF16), vj)
    yb_ref[...] = _unstack_heads(acc / den, B_HEADS, ls).astype(BF16)

    qc = _stack_heads(qc_ref[...], C_HEADS, 256)
    ql = qc[:, :C_KV_RANK]
    qr = qc[:, C_KV_RANK:C_KV_RANK + C_ROPE]
    for j in range(nblk):
        cj = cbuf[slot, j * MOBA_BLOCK:(j + 1) * MOBA_BLOCK, :].astype(BF16)
        rj = jnp.concatenate([rbuf[slot, j * ppb + t] for t in range(ppb)], axis=1).astype(BF16)
        sc_ref[:, j * MOBA_BLOCK:(j + 1) * MOBA_BLOCK] = _dot_nt(ql, cj) + _dot(qr, rj)
    kcn = kcnew_ref[...]
    s_new = jnp.where(causal_new, _dot_nt(qc, kcn), NEG)
    s_all = sc_ref[...]
    m = jnp.maximum(jnp.max(s_all, axis=1, keepdims=True), jnp.max(s_new, axis=1, keepdims=True))
    p_new = jnp.exp(s_new - m)
    sc_ref[...] = jnp.exp(s_all - m)
    den = jnp.sum(sc_ref[...], axis=1, keepdims=True) + jnp.sum(p_new, axis=1, keepdims=True)
    acc = _dot(p_new.astype(BF16), kcn[:, :C_KV_RANK])
    for j in range(nblk):
        cj = cbuf[slot, j * MOBA_BLOCK:(j + 1) * MOBA_BLOCK, :].astype(BF16)
        acc = acc + _dot(sc_ref[:, j * MOBA_BLOCK:(j + 1) * MOBA_BLOCK].astype(BF16), cj)
    o_lat = (acc / den).astype(BF16)
    for h in range(C_HEADS):
        yc_ref[:, h * C_V:(h + 1) * C_V] = _dot(o_lat[h * ls:(h + 1) * ls, :], wuv_ref[h]).astype(BF16)


def _attn_sample(page_table, q_rot3, k_rot3, v3, qcat3, kcat3, w_uv, caches, layer, n_pool, bs, ls):
    n_pages = page_table.shape[1]
    n_past = n_pages * PAGE_SIZE
    nblk = n_past // MOBA_BLOCK
    nblk_pad = -(-nblk // 8) * 8
    rows = B_HEADS * ls
    kern = functools.partial(_attn_sample_kernel, n_pages=n_pages, page_off=layer * n_pool, ls=ls)
    new = lambda w: pl.BlockSpec((None, ls, w), lambda b, pt: (b, 0, 0))
    grid_spec = pltpu.PrefetchScalarGridSpec(
        num_scalar_prefetch=1,
        grid=(bs,),
        in_specs=[new(B_WIDTH), new(HEAD_DIM), new(HEAD_DIM), new(1024), new(256),
                  pl.BlockSpec((None, C_HEADS, C_KV_RANK, C_V), lambda b, pt: (layer, 0, 0, 0)),
                  pl.BlockSpec(memory_space=pl.ANY), pl.BlockSpec(memory_space=pl.ANY),
                  pl.BlockSpec(memory_space=pl.ANY), pl.BlockSpec(memory_space=pl.ANY)],
        out_specs=[new(B_WIDTH), new(C_WIDTH)],
        scratch_shapes=[pltpu.VMEM((2, n_past, HEAD_DIM), F32),
                        pltpu.VMEM((2, n_past, HEAD_DIM), F32),
                        pltpu.VMEM((2, n_past, C_KV_RANK), F32),
                        pltpu.VMEM((2, n_pages, C_ROPE, PAGE_SIZE), F32),
                        pltpu.SemaphoreType.DMA((4, 2)),
                        pltpu.VMEM((nblk_pad, HEAD_DIM), F32),
                        pltpu.VMEM((rows, n_past), F32),
                        pltpu.VMEM((rows, n_past), F32)])
    return pl.pallas_call(
        kern,
        grid_spec=grid_spec,
        out_shape=[jax.ShapeDtypeStruct((bs, ls, B_WIDTH), BF16),
                   jax.ShapeDtypeStruct((bs, ls, C_WIDTH), BF16)],
        compiler_params=_cparams(("arbitrary",)),
        name="attn_sample",
    )(page_table, q_rot3, k_rot3, v3, qcat3, kcat3, w_uv, *caches)


def _rope_tables(pos, d, reps):
    half = d // 2
    inv = ROPE_THETA ** (-jnp.arange(half, dtype=F32) * 2.0 / d)
    ang = pos.astype(F32)[:, None] * inv[None, :]
    cos = jnp.cos(ang)
    sin = jnp.sin(ang)
    return (jnp.tile(jnp.concatenate([cos, cos], axis=1), (1, reps)),
            jnp.tile(jnp.concatenate([-sin, sin], axis=1), (1, reps)))


def _lane_vec(v, lane0):
    depth, heads = v.shape
    out = jnp.zeros((depth, 1, 128), F32)
    return out.at[:, 0, lane0:lane0 + heads].set(v.astype(F32))


def kernel(x_prompt, x_sample, cache_moba_k, cache_moba_v, cache_mla_ckv, cache_mla_krope, state_delta, state_conv, page_table, w_in, w_conv, a_log, dt_bias, g_norm_a, g_q, w_uq, g_kv, w_uk, w_uv, w_o, ln1_g, ln1_b, w_ffn_in, w_ffn_out, ln2_g, ln2_b):
    bp, lp, d = x_prompt.shape
    bs, ls, _ = x_sample.shape
    depth = w_in.shape[0]
    n_pool = cache_moba_k.shape[1]
    n_pages = page_table.shape[1]
    n_past = n_pages * PAGE_SIZE
    n_p = bp * lp
    n_s = bs * ls
    n = n_p + n_s
    assert d == D_MODEL and lp % MOBA_BLOCK == 0 and n_past % MOBA_BLOCK == 0 and ls <= 8
    alpha = (2 * depth) ** 0.25

    tm = math.gcd(math.gcd(n_p, n_s), 512)
    tile_a = 256 if lp % 256 == 0 else DELTA_SUB
    tm_proj = n // 8 if n % 128 == 0 else tm

    qa_end = 4 * A_WIDTH
    o_ba = qa_end
    o_qb = o_ba + 2 * A_HEADS
    o_kb = o_qb + B_WIDTH
    o_vb = o_kb + HEAD_DIM
    o_cq = o_vb + HEAD_DIM
    o_ckv = o_cq + C_Q_RANK
    o_kr = o_ckv + C_KV_RANK
    o_end = o_kr + C_ROPE
    used = OFF_KR + C_ROPE + 2 * A_HEADS
    w_in_p = jnp.concatenate(
        [w_in[..., :qa_end], w_in[..., o_qb:o_kb], w_in[..., o_cq:o_ckv], w_in[..., o_kb:o_vb],
         w_in[..., o_vb:o_cq], w_in[..., o_ckv:o_kr], w_in[..., o_kr:o_end], w_in[..., o_ba:o_qb],
         jnp.zeros(w_in.shape[:2] + (H_PAD - used,), w_in.dtype)], axis=-1).astype(BF16)
    w_uq_p = jnp.concatenate([w_uq[..., :C_NOPE].reshape(depth, C_Q_RANK, C_HEADS * C_NOPE),
                              w_uq[..., C_NOPE:].reshape(depth, C_Q_RANK, C_HEADS * C_ROPE)], axis=-1).astype(BF16)
    w_ukt = jnp.transpose(w_uk, (0, 2, 3, 1)).astype(BF16)
    w_uv_p = jnp.transpose(w_uv, (0, 2, 1, 3)).astype(BF16)
    w_o_b = w_o.astype(BF16)
    w_ffn_in_b = w_ffn_in.astype(BF16)
    w_ffn_out_b = w_ffn_out.astype(BF16)
    alog_vec = _lane_vec(a_log, LANE_DECAY)
    dtb_vec = _lane_vec(dt_bias, LANE_DECAY)
    g_norm3 = g_norm_a.reshape(depth, 1, HEAD_DIM)
    g_q3 = g_q.reshape(depth, 1, C_Q_RANK)
    g_kv3 = g_kv.reshape(depth, 1, C_KV_RANK)
    ln1_g3, ln1_b3 = ln1_g.reshape(depth, 1, d), ln1_b.reshape(depth, 1, d)
    ln2_g3, ln2_b3 = ln2_g.reshape(depth, 1, d), ln2_b.reshape(depth, 1, d)

    pos = jnp.concatenate([jnp.tile(jnp.arange(lp, dtype=jnp.int32), bp),
                           jnp.tile(n_past + jnp.arange(ls, dtype=jnp.int32), bs)])
    tabs = _rope_tables(pos, HEAD_DIM, 1) + _rope_tables(pos, C_ROPE, 4)

    caches = (cache_moba_k.reshape(depth * n_pool, PAGE_SIZE, HEAD_DIM),
              cache_moba_v.reshape(depth * n_pool, PAGE_SIZE, HEAD_DIM),
              cache_mla_ckv.reshape(depth * n_pool, PAGE_SIZE, C_KV_RANK),
              jnp.swapaxes(cache_mla_krope, 2, 3).reshape(depth * n_pool, C_ROPE, PAGE_SIZE))

    x = jnp.concatenate([x_prompt.reshape(n_p, d), x_sample.reshape(n_s, d)], axis=0)
    xb = x.astype(BF16)
    mix = jnp.zeros((n, MIX_WIDTH), BF16)
    rec = []
    for layer in range(depth):
        h = _matmul(xb, w_in_p, layer, tm_proj, H_PAD // 4)
        q_rot, k_rot, ckv_n, kr_rot, qcat, kcat = _prep(h, tabs, g_q3, g_kv3, w_uq_p, w_ukt, layer, tm)
        vb = h[:, OFF_VB:OFF_VB + HEAD_DIM]

        mix, s_p, conv_p = _delta_prompt(h, mix, w_conv, alog_vec, dtb_vec, g_norm3, layer, bp, lp, tile_a)
        mix = _moba_prompt(q_rot, k_rot, vb, mix, bp, lp)
        mix = _mla_prompt(qcat, kcat, w_uv_p, mix, layer, bp, lp, MOBA_BLOCK)

        samp = lambda a: a[n_p:].reshape(bs, ls, a.shape[-1])
        ya_s, s_s, conv_s = _delta_sample(samp(h), state_conv, state_delta, w_conv, alog_vec, dtb_vec, g_norm3,
                                          layer, bs, ls)
        yb_s, yc_s = _attn_sample(page_table, samp(q_rot), samp(k_rot), samp(vb), samp(qcat), samp(kcat),
                                  w_uv_p, caches, layer, n_pool, bs, ls)
        mix_s = jnp.concatenate([ya_s.reshape(n_s, A_WIDTH), yb_s.reshape(n_s, B_WIDTH), yc_s.reshape(n_s, C_WIDTH)],
                                axis=1)
        mix = lax.dynamic_update_slice(mix, mix_s, (n_p, 0))

        x1, x1b = _outproj_ln(mix, w_o_b, x, ln1_g3, ln1_b3, layer, alpha, min(tm, 256))
        x, xb = _ffn_ln(x1b, x1, w_ffn_in_b, w_ffn_out_b, ln2_g3, ln2_b3, layer, alpha, tm, 512)
        rec.append((k_rot, vb, ckv_n, kr_rot, s_p, conv_p, s_s, conv_s))

    def stack(idx, lo, hi, shape):
        return jnp.stack([r[idx][lo:hi].reshape(shape) for r in rec])

    return (x[:n_p].reshape(bp, lp, d), x[n_p:].reshape(bs, ls, d),
            stack(0, 0, n_p, (bp, lp, 1, HEAD_DIM)), stack(1, 0, n_p, (bp, lp, 1, HEAD_DIM)),
            stack(2, 0, n_p, (bp, lp, C_KV_RANK)), stack(3, 0, n_p, (bp, lp, C_ROPE)),
            jnp.stack([r[4] for r in rec]), jnp.stack([r[5] for r in rec]),
            stack(0, n_p, n, (bs, ls, 1, HEAD_DIM)), stack(1, n_p, n, (bs, ls, 1, HEAD_DIM)),
            stack(2, n_p, n, (bs, ls, C_KV_RANK)), stack(3, n_p, n, (bs, ls, C_ROPE)),
            jnp.stack([r[6] for r in rec]), jnp.stack([r[7] for r in rec]))
```

```python
import functools
import math

import jax
import jax.numpy as jnp
from jax import lax
from jax.experimental import pallas as pl
from jax.experimental.pallas import tpu as pltpu

F32 = jnp.float32
BF16 = jnp.bfloat16
HIGHEST = lax.Precision.HIGHEST

D_MODEL = 2048
PAGE_SIZE = 128
HEAD_DIM = 128
A_HEADS = 8
A_WIDTH = A_HEADS * HEAD_DIM
CONV_W = 4
DELTA_CHUNK = 64
B_HEADS = 4
B_WIDTH = B_HEADS * HEAD_DIM
MOBA_BLOCK = 256
MOBA_TOPK = 3
C_HEADS = 4
C_NOPE = 128
C_ROPE = 64
C_V = 128
C_Q_RANK = 384
C_KV_RANK = 128
C_WIDTH = C_HEADS * C_V
MIX_WIDTH = A_WIDTH + B_WIDTH + C_WIDTH
ROPE_THETA = 10000.0
LN_EPS = 1e-5
RMS_EPS = 1e-6
L2_EPS = 1e-6

OFF_QKV = 0
OFF_Z = 3 * A_WIDTH
OFF_QB = 4 * A_WIDTH
OFF_CQ = OFF_QB + B_WIDTH
OFF_KB = OFF_CQ + C_Q_RANK
OFF_VB = OFF_KB + HEAD_DIM
OFF_CKV = OFF_VB + HEAD_DIM
OFF_KR = OFF_CKV + C_KV_RANK
H_PAD = 5632
LANE_BETA = C_ROPE
LANE_DECAY = C_ROPE + A_HEADS

NEG = -1e30
VMEM_LIMIT = 56 * 1024 * 1024
DELTA_SUB = 128


def _cparams(sem):
    return pltpu.CompilerParams(dimension_semantics=sem, vmem_limit_bytes=VMEM_LIMIT)


def _dot(a, b):
    return jnp.dot(a, b, preferred_element_type=F32)


def _dot_exact(a, b):
    return jnp.dot(a, b, preferred_element_type=F32, precision=HIGHEST)


def _dot_nt(a, b, precision=None):
    return lax.dot_general(a, b, (((1,), (1,)), ((), ())), preferred_element_type=F32, precision=precision)


def _dot_tn(a, b, precision=None):
    return lax.dot_general(a, b, (((0,), (0,)), ((), ())), preferred_element_type=F32, precision=precision)


def _dot_b(a, b):
    return _dot(a.astype(BF16), b.astype(BF16))


def _matmul_kernel(x_ref, w_ref, o_ref):
    o_ref[...] = _dot(x_ref[...], w_ref[...])


def _matmul(xb, w, layer, tm, tn):
    m, k = xb.shape
    n = w.shape[-1]
    return pl.pallas_call(
        _matmul_kernel,
        grid=(m // tm, n // tn),
        in_specs=[pl.BlockSpec((tm, k), lambda i, j: (i, 0)),
                  pl.BlockSpec((None, k, tn), lambda i, j: (layer, 0, j))],
        out_specs=pl.BlockSpec((tm, tn), lambda i, j: (i, j)),
        out_shape=jax.ShapeDtypeStruct((m, n), F32),
        compiler_params=_cparams(("parallel", "arbitrary")),
        name="proj_in",
    )(xb, w)


def _layer_norm_rows(y, g, b):
    mu = jnp.mean(y, axis=-1, keepdims=True)
    yc = y - mu
    var = jnp.mean(yc * yc, axis=-1, keepdims=True)
    return yc * lax.rsqrt(var + LN_EPS) * g + b


def _outproj_ln_kernel(mix_ref, w_ref, x_ref, g_ref, b_ref, o_ref, ob_ref, *, alpha):
    y = alpha * x_ref[...] + _dot(mix_ref[...], w_ref[...])
    out = _layer_norm_rows(y, g_ref[...], b_ref[...])
    o_ref[...] = out
    ob_ref[...] = out.astype(BF16)


def _outproj_ln(mix, w_o, x, g, b, layer, alpha, tm):
    m, d = x.shape
    kdim = mix.shape[1]
    return pl.pallas_call(
        functools.partial(_outproj_ln_kernel, alpha=alpha),
        grid=(m // tm,),
        in_specs=[pl.BlockSpec((tm, kdim), lambda i: (i, 0)),
                  pl.BlockSpec((None, kdim, d), lambda i: (layer, 0, 0)),
                  pl.BlockSpec((tm, d), lambda i: (i, 0)),
                  pl.BlockSpec((None, 1, d), lambda i: (layer, 0, 0)),
                  pl.BlockSpec((None, 1, d), lambda i: (layer, 0, 0))],
        out_specs=[pl.BlockSpec((tm, d), lambda i: (i, 0)),
                   pl.BlockSpec((tm, d), lambda i: (i, 0))],
        out_shape=[jax.ShapeDtypeStruct((m, d), F32), jax.ShapeDtypeStruct((m, d), BF16)],
        compiler_params=_cparams(("parallel",)),
        name="outproj_ln",
    )(mix, w_o, x, g, b)


def _ffn_ln_kernel(xb_ref, wg_ref, wu_ref, wo_ref, x_ref, g_ref, b_ref, o_ref, ob_ref, acc_ref, *, alpha):
    f = pl.program_id(1)

    @pl.when(f == 0)
    def _():
        acc_ref[...] = jnp.zeros_like(acc_ref)

    xb = xb_ref[...]
    gate = _dot(xb, wg_ref[...])
    up = _dot(xb, wu_ref[...])
    act = (jax.nn.silu(gate) * up).astype(BF16)
    acc_ref[...] += _dot(act, wo_ref[...])

    @pl.when(f == pl.num_programs(1) - 1)
    def _():
        y = alpha * x_ref[...] + acc_ref[...]
        out = _layer_norm_rows(y, g_ref[...], b_ref[...])
        o_ref[...] = out
        ob_ref[...] = out.astype(BF16)


def _ffn_ln(xb, x, w_in, w_out, g, b, layer, alpha, tm, tf):
    m, d = x.shape
    d_ff = w_out.shape[1]
    nf = d_ff // tf
    return pl.pallas_call(
        functools.partial(_ffn_ln_kernel, alpha=alpha),
        grid=(m // tm, nf),
        in_specs=[pl.BlockSpec((tm, d), lambda i, f: (i, 0)),
                  pl.BlockSpec((None, d, tf), lambda i, f: (layer, 0, f)),
                  pl.BlockSpec((None, d, tf), lambda i, f: (layer, 0, nf + f)),
                  pl.BlockSpec((None, tf, d), lambda i, f: (layer, f, 0)),
                  pl.BlockSpec((tm, d), lambda i, f: (i, 0)),
                  pl.BlockSpec((None, 1, d), lambda i, f: (layer, 0, 0)),
                  pl.BlockSpec((None, 1, d), lambda i, f: (layer, 0, 0))],
        out_specs=[pl.BlockSpec((tm, d), lambda i, f: (i, 0)),
                   pl.BlockSpec((tm, d), lambda i, f: (i, 0))],
        out_shape=[jax.ShapeDtypeStruct((m, d), F32), jax.ShapeDtypeStruct((m, d), BF16)],
        scratch_shapes=[pltpu.VMEM((tm, d), F32)],
        compiler_params=_cparams(("parallel", "arbitrary")),
        name="ffn_ln",
    )(xb, w_in, w_in, w_out, x, g, b)


def _swap_half64(a):
    lane = lax.broadcasted_iota(jnp.int32, a.shape, 1)
    first = jnp.bitwise_and(lane, 63) < 32
    return jnp.where(first, pltpu.roll(a, 96, 1), pltpu.roll(a, 32, 1))


def _prep_kernel(qb_ref, cq_ref, kb_ref, ckv_ref, kr_ref, cos_ref, sin_ref, cos64_ref, sin64_ref,
                 gq_ref, gkv_ref, wuq_ref, wukt_ref,
                 qrot_ref, krot_ref, ckvn_ref, krr_ref, qcat_ref, kcat_ref):
    cos = cos_ref[...]
    sin = sin_ref[...]
    q = qb_ref[...]
    for h in range(B_HEADS):
        xs = q[:, h * HEAD_DIM:(h + 1) * HEAD_DIM]
        qrot_ref[:, h * HEAD_DIM:(h + 1) * HEAD_DIM] = xs * cos + pltpu.roll(xs, HEAD_DIM // 2, 1) * sin
    k = kb_ref[...]
    krot_ref[...] = k * cos + pltpu.roll(k, HEAD_DIM // 2, 1) * sin

    cq = cq_ref[...]
    cqn = cq * lax.rsqrt(jnp.mean(cq * cq, axis=-1, keepdims=True) + RMS_EPS) * gq_ref[...]
    qfull = _dot(cqn.astype(BF16), wuq_ref[...])
    c64 = cos64_ref[...]
    s64 = sin64_ref[...]
    nope_w = C_HEADS * C_NOPE
    halves = []
    for half in range(2):
        a = qfull[:, nope_w + half * 128: nope_w + (half + 1) * 128]
        halves.append(a * c64[:, half * 128:(half + 1) * 128] + _swap_half64(a) * s64[:, half * 128:(half + 1) * 128])
    scale = (C_NOPE + C_ROPE) ** -0.5
    zpad = jnp.zeros((q.shape[0], 64), F32)
    for h in range(C_HEADS):
        ql = _dot(qfull[:, h * C_NOPE:(h + 1) * C_NOPE].astype(BF16), wukt_ref[h])
        hr = halves[h // 2][:, (h % 2) * C_ROPE:(h % 2 + 1) * C_ROPE]
        qcat_ref[:, h * 256:(h + 1) * 256] = (jnp.concatenate([ql, hr, zpad], axis=1) * scale).astype(BF16)

    ck = ckv_ref[...]
    ckn = ck * lax.rsqrt(jnp.mean(ck * ck, axis=-1, keepdims=True) + RMS_EPS) * gkv_ref[...]
    ckvn_ref[...] = ckn
    krb = kr_ref[...]
    krr = (krb * c64[:, :128] + _swap_half64(krb) * s64[:, :128])[:, :C_ROPE]
    krr_ref[...] = krr
    kcat_ref[...] = jnp.concatenate([ckn, krr, zpad], axis=1).astype(BF16)


def _prep(h, tabs, g_q, g_kv, w_uq, w_ukt, layer, tm):
    n = h.shape[0]
    cos128, sin128, cos64, sin64 = tabs
    row = lambda w, off: pl.BlockSpec((tm, w), lambda i: (i, off // w))
    tab = lambda w: pl.BlockSpec((tm, w), lambda i: (i, 0))
    return pl.pallas_call(
        _prep_kernel,
        grid=(n // tm,),
        in_specs=[row(B_WIDTH, OFF_QB), row(C_Q_RANK, OFF_CQ), row(HEAD_DIM, OFF_KB),
                  row(C_KV_RANK, OFF_CKV), row(128, OFF_KR),
                  tab(128), tab(128), tab(256), tab(256),
                  pl.BlockSpec((None, 1, C_Q_RANK), lambda i: (layer, 0, 0)),
                  pl.BlockSpec((None, 1, C_KV_RANK), lambda i: (layer, 0, 0)),
                  pl.BlockSpec((None, C_Q_RANK, 768), lambda i: (layer, 0, 0)),
                  pl.BlockSpec((None, C_HEADS, C_NOPE, C_KV_RANK), lambda i: (layer, 0, 0, 0))],
        out_specs=[tab(B_WIDTH), tab(HEAD_DIM), tab(C_KV_RANK), tab(C_ROPE), tab(1024), tab(256)],
        out_shape=[jax.ShapeDtypeStruct((n, B_WIDTH), F32),
                   jax.ShapeDtypeStruct((n, HEAD_DIM), F32),
                   jax.ShapeDtypeStruct((n, C_KV_RANK), F32),
                   jax.ShapeDtypeStruct((n, C_ROPE), F32),
                   jax.ShapeDtypeStruct((n, 1024), BF16),
                   jax.ShapeDtypeStruct((n, 256), BF16)],
        compiler_params=_cparams(("parallel",)),
        name="prep",
    )(h, h, h, h, h, cos128, sin128, cos64, sin64, g_q, g_kv, w_uq, w_ukt)


def _inv_unit_lower(mats, chunk):
    n = mats[0].shape[0]
    r = lax.broadcasted_iota(jnp.int32, (n, n), 0)
    col = lax.broadcasted_iota(jnp.int32, (n, n), 1)
    eye = (r == col).astype(F32)
    blk8 = jnp.right_shift(r, 3) == jnp.right_shift(col, 3)
    a8f = [jnp.where(blk8, a, 0.0) for a in mats]
    a8 = [a.astype(BF16) for a in a8f]
    xs = [eye - a for a in a8f]
    ps = [_dot(a, a) for a in a8]
    xs = [x + _dot_b(x, p) for x, p in zip(xs, ps)]
    ps = [_dot_b(p, p) for p in ps]
    xs = [x + _dot_b(x, p) for x, p in zip(xs, ps)]
    k = 8
    while k < chunk:
        sh = k.bit_length() - 1
        same2k = jnp.right_shift(r, sh + 1) == jnp.right_shift(col, sh + 1)
        samek = jnp.right_shift(r, sh) == jnp.right_shift(col, sh)
        off = same2k & jnp.logical_not(samek)
        ms = [jnp.where(off, a, 0.0).astype(BF16) for a in mats]
        xb = [x.astype(BF16) for x in xs]
        ts = [_dot(x, m) for x, m in zip(xb, ms)]
        xs = [x - _dot(t.astype(BF16), xh) for x, t, xh in zip(xs, ts, xb)]
        k *= 2
    return xs


def _wy_prepare(systems, chunk):
    n = systems[0][0].shape[0]
    r = lax.broadcasted_iota(jnp.int32, (n, n), 0)
    col = lax.broadcasted_iota(jnp.int32, (n, n), 1)
    sh = chunk.bit_length() - 1
    incl = (jnp.right_shift(r, sh) == jnp.right_shift(col, sh)) & (r >= col)
    strict = r > col
    kbs = [s[1].astype(BF16) for s in systems]
    kks = [_dot_nt(kb, kb) for kb in kbs]
    qks = [_dot_nt(s[0].astype(BF16), kb) for s, kb in zip(systems, kbs)]
    decays = [jnp.exp(jnp.where(incl, s[3] - s[4], -jnp.inf)) for s in systems]
    mats = [jnp.where(strict, s[5] * kk * dc, 0.0) for s, kk, dc in zip(systems, kks, decays)]
    tinvs = _inv_unit_lower(mats, chunk)
    egs = [jnp.exp(s[3]) for s in systems]
    rhs = [jnp.concatenate([s[5] * s[2], (s[5] * eg) * s[1]], axis=1) for s, eg in zip(systems, egs)]
    sols = [_dot_b(t, x) for t, x in zip(tinvs, rhs)]
    return [(sol[:, :HEAD_DIM], sol[:, HEAD_DIM:], s[0] * eg, qk * dc)
            for sol, s, eg, qk, dc in zip(sols, systems, egs, qks, decays)]


def _l2norm_rows(x):
    return x * lax.rsqrt(jnp.sum(x * x, axis=-1, keepdims=True) + L2_EPS)


def _decay_and_beta(bb, alog_ref, dtb_ref, valid):
    lane = lax.broadcasted_iota(jnp.int32, bb.shape, 1)
    is_decay = (lane >= LANE_DECAY) & (lane < LANE_DECAY + A_HEADS)
    g = -jnp.exp(alog_ref[...]) * jax.nn.softplus(bb + dtb_ref[...])
    g = jnp.where(is_decay & valid, g, 0.0)
    beta = jnp.where(valid, jax.nn.sigmoid(bb), 0.0)
    return g, beta


def _gated_out(o, z, gnorm):
    on = o * lax.rsqrt(jnp.mean(o * o, axis=-1, keepdims=True) + RMS_EPS) * gnorm
    return (on * jax.nn.silu(z)).astype(BF16)


def _delta_prompt_kernel(u_ref, z_ref, ba_ref, wconv_ref, alog_ref, dtb_ref, gnorm_ref, mix_hbm,
                         ya_ref, sfin_ref, cfin_ref,
                         ubuf, s_ref, qn_ref, kn_ref, vn_ref, u_s, wq_s, qkd_s, o_s, d_s, *, tile, chunk):
    del mix_hbm
    t = pl.program_id(1)
    nchunk = tile // chunk
    nsub = tile // DELTA_SUB

    @pl.when(t == 0)
    def _():
        ubuf[0:8, :] = jnp.zeros((8, ubuf.shape[1]), F32)
        s_ref[...] = jnp.zeros_like(s_ref)

    ubuf[8:8 + tile, :] = u_ref[...]
    for grp in range(3 * A_HEADS):
        lo = grp * HEAD_DIM
        y = ubuf[5:5 + tile, lo:lo + HEAD_DIM] * wconv_ref[0:1, lo:lo + HEAD_DIM]
        for i in range(1, CONV_W):
            y = y + ubuf[5 + i:5 + i + tile, lo:lo + HEAD_DIM] * wconv_ref[i:i + 1, lo:lo + HEAD_DIM]
        y = jax.nn.silu(y)
        hh = grp % A_HEADS
        if grp < A_HEADS:
            qn_ref[hh] = _l2norm_rows(y) * (HEAD_DIM ** -0.5)
        elif grp < 2 * A_HEADS:
            kn_ref[hh] = _l2norm_rows(y)
        else:
            vn_ref[hh] = y
    cfin_ref[...] = ubuf[tile + 5:tile + 8, :]
    ubuf[0:8, :] = ubuf[tile:tile + 8, :]

    g, beta = _decay_and_beta(ba_ref[...], alog_ref, dtb_ref, True)
    r = lax.broadcasted_iota(jnp.int32, (tile, tile), 0)
    col = lax.broadcasted_iota(jnp.int32, (tile, tile), 1)
    sh = chunk.bit_length() - 1
    same = jnp.right_shift(r, sh) == jnp.right_shift(col, sh)
    g_cum = _dot_exact(jnp.where(same & (r >= col), 1.0, 0.0), g)
    gt_cum = _dot_exact(g.T, jnp.where(same & (r <= col), 1.0, 0.0))

    where = [(hh, sb * DELTA_SUB) for hh in range(A_HEADS) for sb in range(nsub)]
    systems = []
    for hh, lo in where:
        hi = lo + DELTA_SUB
        ld = LANE_DECAY + hh
        lb = LANE_BETA + hh
        systems.append((qn_ref[hh, lo:hi, :], kn_ref[hh, lo:hi, :], vn_ref[hh, lo:hi, :],
                        g_cum[lo:hi, ld:ld + 1], gt_cum[ld:ld + 1, lo:hi], beta[lo:hi, lb:lb + 1]))
    for (hh, lo), (u, w, qe, qkd) in zip(where, _wy_prepare(systems, chunk)):
        u_s[hh, lo:lo + DELTA_SUB, :] = u
        for c in range(DELTA_SUB // chunk):
            a0 = lo + c * chunk
            wq_s[hh, 2 * a0:2 * a0 + chunk, :] = w[c * chunk:(c + 1) * chunk, :].astype(BF16)
            wq_s[hh, 2 * a0 + chunk:2 * a0 + 2 * chunk, :] = qe[c * chunk:(c + 1) * chunk, :].astype(BF16)
        qkd_s[hh, lo:lo + DELTA_SUB, :] = qkd.astype(BF16)

    for c in range(nchunk):
        lo, hi = c * chunk, (c + 1) * chunk
        for hh in range(A_HEADS):
            ld = LANE_DECAY + hh
            s = s_ref[hh]
            ws = _dot(wq_s[hh, 2 * lo:2 * hi, :], s.astype(BF16))
            delta = (u_s[hh, lo:hi, :] - ws[:chunk, :]).astype(BF16)
            o_s[hh, lo:hi, :] = ws[chunk:, :]
            d_s[hh, lo:hi, :] = delta
            g_last = g_cum[hi - 1:hi, ld:ld + 1]
            kd = kn_ref[hh, lo:hi, :] * jnp.exp(g_last - g_cum[lo:hi, ld:ld + 1])
            s_ref[hh] = s * jnp.exp(g_last) + _dot_tn(kd.astype(BF16), delta)

    gnorm = gnorm_ref[...]
    for hh in range(A_HEADS):
        for sb in range(nsub):
            lo, hi = sb * DELTA_SUB, (sb + 1) * DELTA_SUB
            o = o_s[hh, lo:hi, :] + _dot(qkd_s[hh, lo:hi, :], d_s[hh, lo:hi, :])
            ya_ref[lo:hi, hh * HEAD_DIM:(hh + 1) * HEAD_DIM] = _gated_out(
                o, z_ref[lo:hi, hh * HEAD_DIM:(hh + 1) * HEAD_DIM], gnorm)

    @pl.when(t == pl.num_programs(1) - 1)
    def _():
        sfin_ref[...] = s_ref[...]


def _delta_prompt(h, mix, w_conv, alog_vec, dtb_vec, g_norm, layer, bp, lp, tile):
    nt = lp // tile
    qkv_w = 3 * A_WIDTH
    kern = functools.partial(_delta_prompt_kernel, tile=tile, chunk=DELTA_CHUNK)
    head_buf = lambda rows, dt: pltpu.VMEM((A_HEADS, rows, HEAD_DIM), dt)
    return pl.pallas_call(
        kern,
        grid=(bp, nt),
        in_specs=[pl.BlockSpec((tile, qkv_w), lambda b, t: (b * nt + t, 0)),
                  pl.BlockSpec((tile, A_WIDTH), lambda b, t: (b * nt + t, OFF_Z // A_WIDTH)),
                  pl.BlockSpec((tile, 128), lambda b, t: (b * nt + t, OFF_KR // 128)),
                  pl.BlockSpec((None, CONV_W, qkv_w), lambda b, t: (layer, 0, 0)),
                  pl.BlockSpec((None, 1, 128), lambda b, t: (layer, 0, 0)),
                  pl.BlockSpec((None, 1, 128), lambda b, t: (layer, 0, 0)),
                  pl.BlockSpec((None, 1, HEAD_DIM), lambda b, t: (layer, 0, 0)),
                  pl.BlockSpec(memory_space=pl.ANY)],
        out_specs=[pl.BlockSpec((tile, A_WIDTH), lambda b, t: (b * nt + t, 0)),
                   pl.BlockSpec((None, A_HEADS, HEAD_DIM, HEAD_DIM), lambda b, t: (b, 0, 0, 0)),
                   pl.BlockSpec((None, CONV_W - 1, qkv_w), lambda b, t: (b, 0, 0))],
        out_shape=[jax.ShapeDtypeStruct(mix.shape, BF16),
                   jax.ShapeDtypeStruct((bp, A_HEADS, HEAD_DIM, HEAD_DIM), F32),
                   jax.ShapeDtypeStruct((bp, CONV_W - 1, qkv_w), F32)],
        scratch_shapes=[pltpu.VMEM((tile + 8, qkv_w), F32),
                        head_buf(HEAD_DIM, F32),
                        head_buf(tile, F32), head_buf(tile, F32), head_buf(tile, F32),
                        head_buf(tile, F32), head_buf(2 * tile, BF16), head_buf(tile, BF16),
                        head_buf(tile, F32), head_buf(tile, BF16)],
        input_output_aliases={7: 0},
        compiler_params=_cparams(("parallel", "arbitrary")),
        name="delta_prompt",
    )(h, h, h, w_conv, alog_vec, dtb_vec, g_norm, mix)


def _delta_sample_kernel(u_ref, z_ref, ba_ref, cs_ref, s0_ref, wconv_ref, alog_ref, dtb_ref, gnorm_ref, sall_hbm,
                         ya_ref, snew_ref, cnew_ref, buf, *, ls, group):
    del sall_hbm
    rows = 8
    width = buf.shape[2]
    heads = range(A_HEADS)
    rid = lax.broadcasted_iota(jnp.int32, (rows, 128), 0)
    r = lax.broadcasted_iota(jnp.int32, (rows, rows), 0)
    col = lax.broadcasted_iota(jnp.int32, (rows, rows), 1)
    tril = jnp.where(r >= col, 1.0, 0.0)

    def conv_group(gi, grp):
        lo = grp * HEAD_DIM
        y = buf[gi, 0:rows, lo:lo + HEAD_DIM] * wconv_ref[0:1, lo:lo + HEAD_DIM]
        for i in range(1, CONV_W):
            y = y + buf[gi, i:i + rows, lo:lo + HEAD_DIM] * wconv_ref[i:i + 1, lo:lo + HEAD_DIM]
        return jax.nn.silu(y)

    systems = []
    for gi in range(group):
        buf[gi, 0:CONV_W - 1, :] = cs_ref[gi]
        buf[gi, CONV_W - 1:CONV_W - 1 + ls, :] = u_ref[gi]
        buf[gi, CONV_W - 1 + ls:, :] = jnp.zeros((buf.shape[1] - (CONV_W - 1 + ls), width), F32)
        cnew_ref[gi] = buf[gi, ls:ls + CONV_W - 1, :]
        bb = jnp.concatenate([ba_ref[gi], jnp.zeros((rows - ls, 128), F32)], axis=0)
        g, beta = _decay_and_beta(bb, alog_ref, dtb_ref, rid < ls)
        g_cum = _dot_exact(tril, g)
        q = jnp.concatenate([_l2norm_rows(conv_group(gi, hh)) * (HEAD_DIM ** -0.5) for hh in heads], axis=0)
        k = jnp.concatenate([_l2norm_rows(conv_group(gi, A_HEADS + hh)) for hh in heads], axis=0)
        v = jnp.concatenate([conv_group(gi, 2 * A_HEADS + hh) for hh in heads], axis=0)
        g_col = jnp.concatenate([g_cum[:, LANE_DECAY + hh:LANE_DECAY + hh + 1] for hh in heads], axis=0)
        beta_col = jnp.concatenate([beta[:, LANE_BETA + hh:LANE_BETA + hh + 1] for hh in heads], axis=0)
        g_row = jnp.broadcast_to(g_col, (A_HEADS * rows, 128)).T[0:1, :]
        systems.append((q, k, v, g_col, g_row, beta_col))
    prepared = _wy_prepare(systems, rows)

    nrow = A_HEADS * rows
    wide = A_HEADS * HEAD_DIM
    row_head = jnp.bitwise_and(jnp.right_shift(lax.broadcasted_iota(jnp.int32, (2 * nrow, wide), 0), 3), A_HEADS - 1)
    col_head = jnp.right_shift(lax.broadcasted_iota(jnp.int32, (2 * nrow, wide), 1), 7)
    own_head = row_head == col_head
    gnorm = gnorm_ref[...]
    for gi in range(group):
        u, w, qe, qkd = prepared[gi]
        k, g_col = systems[gi][1], systems[gi][3]
        s_stack = s0_ref[gi].reshape(wide, HEAD_DIM)
        lhs = jnp.tile(jnp.concatenate([w, qe], axis=0), (1, A_HEADS))
        ws = _dot(jnp.where(own_head, lhs, 0.0).astype(BF16), s_stack.astype(BF16))
        delta = u - ws[:nrow, :]
        o = ws[nrow:, :] + _dot_b(qkd, delta)
        g_last = jnp.concatenate([jnp.broadcast_to(g_col[(hh + 1) * rows - 1:(hh + 1) * rows, :], (rows, 1))
                                  for hh in heads], axis=0)
        kd = k * jnp.exp(g_last - g_col)
        d_wide = jnp.where(own_head[:nrow, :], jnp.tile(delta, (1, A_HEADS)), 0.0)
        upd = _dot_tn(kd.astype(BF16), d_wide.astype(BF16))
        for hh in heads:
            lo = hh * HEAD_DIM
            decay_h = jnp.exp(g_col[(hh + 1) * rows - 1:(hh + 1) * rows, :])
            snew_ref[gi, hh] = s0_ref[gi, hh] * decay_h + upd[:, lo:lo + HEAD_DIM]
            ya_ref[gi, :, lo:lo + HEAD_DIM] = _gated_out(o[hh * rows:hh * rows + ls, :],
                                                          z_ref[gi, :, lo:lo + HEAD_DIM], gnorm)


def _delta_sample(h3, state_conv, state_delta, s_all, w_conv, alog_vec, dtb_vec, g_norm, layer, bs, ls):
    qkv_w = 3 * A_WIDTH
    group = 4 if bs % 4 == 0 else 1
    kern = functools.partial(_delta_sample_kernel, ls=ls, group=group)
    return pl.pallas_call(
        kern,
        grid=(bs // group,),
        in_specs=[pl.BlockSpec((group, ls, qkv_w), lambda b: (b, 0, 0)),
                  pl.BlockSpec((group, ls, A_WIDTH), lambda b: (b, 0, OFF_Z // A_WIDTH)),
                  pl.BlockSpec((group, ls, 128), lambda b: (b, 0, OFF_KR // 128)),
                  pl.BlockSpec((None, group, CONV_W - 1, qkv_w), lambda b: (layer, b, 0, 0)),
                  pl.BlockSpec((None, group, A_HEADS, HEAD_DIM, HEAD_DIM), lambda b: (layer, b, 0, 0, 0)),
                  pl.BlockSpec((None, CONV_W, qkv_w), lambda b: (layer, 0, 0)),
                  pl.BlockSpec((None, 1, 128), lambda b: (layer, 0, 0)),
                  pl.BlockSpec((None, 1, 128), lambda b: (layer, 0, 0)),
                  pl.BlockSpec((None, 1, HEAD_DIM), lambda b: (layer, 0, 0)),
                  pl.BlockSpec(memory_space=pl.ANY)],
        out_specs=[pl.BlockSpec((group, ls, A_WIDTH), lambda b: (b, 0, 0)),
                   pl.BlockSpec((None, group, A_HEADS, HEAD_DIM, HEAD_DIM), lambda b: (layer, b, 0, 0, 0)),
                   pl.BlockSpec((group, CONV_W - 1, qkv_w), lambda b: (b, 0, 0))],
        out_shape=[jax.ShapeDtypeStruct((bs, ls, A_WIDTH), BF16),
                   jax.ShapeDtypeStruct(s_all.shape, F32),
                   jax.ShapeDtypeStruct((bs, CONV_W - 1, qkv_w), F32)],
        scratch_shapes=[pltpu.VMEM((group, 16, qkv_w), F32)],
        input_output_aliases={9: 1},
        compiler_params=_cparams(("parallel",)),
        name="delta_sample",
    )(h3, h3, h3, state_conv, state_delta, w_conv, alog_vec, dtb_vec, g_norm, s_all)


def _top_blocks(gate, n_valid, axis):
    blk = lax.broadcasted_iota(jnp.int32, gate.shape, axis)
    big = jnp.int32(2 ** 30)
    cand = blk < n_valid
    g = jnp.where(cand, gate, -jnp.inf)
    picks = []
    for _ in range(MOBA_TOPK):
        mx = jnp.max(g, axis=axis, keepdims=True)
        first = jnp.min(jnp.where((g == mx) & cand, blk, big), axis=axis, keepdims=True)
        picks.append(jnp.where(first == big, -1, first))
        hit = blk == first
        cand = cand & jnp.logical_not(hit)
        g = jnp.where(hit, -jnp.inf, g)
    return picks


def _block_mean(x):
    rows = x.shape[0]
    part = jnp.sum(x.reshape(rows // 8, 8, x.shape[1]), axis=0)
    return jnp.sum(part, axis=0, keepdims=True) * (1.0 / rows)


def _stack_heads(x, heads, width):
    return jnp.concatenate([x[:, h * width:(h + 1) * width] for h in range(heads)], axis=0)


def _unstack_heads(x, heads, rows):
    return jnp.concatenate([x[h * rows:(h + 1) * rows, :] for h in range(heads)], axis=1)


def _flash_init(s, m_ref, l_ref, acc_ref, vt):
    m0 = jnp.max(s, axis=0, keepdims=True)
    p = jnp.exp(s - m0)
    m_ref[...] = m0
    l_ref[...] = jnp.sum(p, axis=0, keepdims=True)
    acc_ref[...] = _dot(vt, p.astype(BF16))


def _flash_step(blocks, m_ref, l_ref, acc_ref):
    m_old = m_ref[...]
    m_new = m_old
    for s, _ in blocks:
        m_new = jnp.maximum(m_new, jnp.max(s, axis=0, keepdims=True))
    alpha = jnp.exp(m_old - m_new)
    l_new = alpha * l_ref[...]
    acc = alpha * acc_ref[...]
    for s, vt in blocks:
        p = jnp.exp(s - m_new)
        l_new = l_new + jnp.sum(p, axis=0, keepdims=True)
        acc = acc + _dot(vt, p.astype(BF16))
    l_ref[...] = l_new
    acc_ref[...] = acc
    m_ref[...] = m_new


def _loop_in_pairs(n, block):
    def pair(jp, carry):
        block([2 * jp, 2 * jp + 1])
        return carry

    lax.fori_loop(0, lax.div(n, 2), pair, 0)

    @pl.when(lax.rem(n, 2) == 1)
    def _():
        block([n - 1])


def _causal_t(s, tq):
    kpos = lax.broadcasted_iota(jnp.int32, s.shape, 0)
    qpos = jnp.bitwise_and(lax.broadcasted_iota(jnp.int32, s.shape, 1), tq - 1)
    return jnp.where(kpos <= qpos, s, NEG)


def _moba_prompt_kernel(q_ref, k_ref, v_ref, mix_hbm, o_ref, kmean_ref, kb_ref, vt_ref, m_ref, l_ref, acc_ref, *, nblk):
    del mix_hbm
    i = pl.program_id(1)
    tq = MOBA_BLOCK

    @pl.when(i == 0)
    def _():
        kmean_ref[...] = jnp.zeros_like(kmean_ref)
        for j in range(nblk):
            kj = k_ref[j * tq:(j + 1) * tq, :]
            kmean_ref[j:j + 1, :] = _block_mean(kj)
            kb_ref[j] = kj.astype(BF16)
            vt_ref[j] = v_ref[j * tq:(j + 1) * tq, :].T.astype(BF16)

    qs = _stack_heads(q_ref[...], B_HEADS, HEAD_DIM)
    qb = (qs * (HEAD_DIM ** -0.5)).astype(BF16)
    gate = _dot_nt(kmean_ref[...], qs, precision=HIGHEST)
    picks = _top_blocks(gate, i, 0)

    _flash_init(_causal_t(_dot_nt(kb_ref[i], qb), tq), m_ref, l_ref, acc_ref, vt_ref[i])

    def past_blocks(ids):
        blocks = []
        for j in ids:
            chosen = (picks[0] == j) | (picks[1] == j) | (picks[2] == j)
            blocks.append((jnp.where(chosen, _dot_nt(kb_ref[j], qb), NEG), vt_ref[j]))
        _flash_step(blocks, m_ref, l_ref, acc_ref)

    _loop_in_pairs(i, past_blocks)
    out_t = acc_ref[...] / l_ref[...]
    for h in range(B_HEADS):
        o_ref[:, h * HEAD_DIM:(h + 1) * HEAD_DIM] = out_t[:, h * tq:(h + 1) * tq].T.astype(BF16)


def _moba_prompt(q_rot, k_rot, v, mix, bp, lp):
    tq = MOBA_BLOCK
    nq = lp // tq
    nblk = lp // MOBA_BLOCK
    nblk_pad = -(-nblk // 8) * 8
    kern = functools.partial(_moba_prompt_kernel, nblk=nblk)
    return pl.pallas_call(
        kern,
        grid=(bp, nq),
        in_specs=[pl.BlockSpec((tq, B_WIDTH), lambda b, i: (b * nq + i, 0)),
                  pl.BlockSpec((lp, HEAD_DIM), lambda b, i: (b, 0)),
                  pl.BlockSpec((lp, HEAD_DIM), lambda b, i: (b, 0)),
                  pl.BlockSpec(memory_space=pl.ANY)],
        out_specs=pl.BlockSpec((tq, B_WIDTH), lambda b, i: (b * nq + i, A_WIDTH // B_WIDTH)),
        out_shape=jax.ShapeDtypeStruct(mix.shape, BF16),
        scratch_shapes=[pltpu.VMEM((nblk_pad, HEAD_DIM), F32),
                        pltpu.VMEM((nblk, tq, HEAD_DIM), BF16),
                        pltpu.VMEM((nblk, HEAD_DIM, tq), BF16),
                        pltpu.VMEM((1, B_HEADS * tq), F32),
                        pltpu.VMEM((1, B_HEADS * tq), F32),
                        pltpu.VMEM((HEAD_DIM, B_HEADS * tq), F32)],
        input_output_aliases={3: 0},
        compiler_params=_cparams(("parallel", "arbitrary")),
        name="moba_prompt",
    )(q_rot, k_rot, v, mix)


def _mla_prompt_kernel(q_ref, kc_ref, wuv_ref, mix_hbm, o_ref, ct_ref, m_ref, l_ref, acc_ref, *, tq, nblk):
    del mix_hbm
    i = pl.program_id(1)

    @pl.when(i == 0)
    def _():
        for j in range(nblk):
            ct_ref[j] = kc_ref[j * tq:(j + 1) * tq, :C_KV_RANK].astype(F32).T.astype(BF16)

    qs = _stack_heads(q_ref[...], C_HEADS, 256)
    r0 = pl.multiple_of(i * tq, tq)
    _flash_init(_causal_t(_dot_nt(kc_ref[pl.ds(r0, tq), :], qs), tq), m_ref, l_ref, acc_ref, ct_ref[i])

    def past_blocks(ids):
        blocks = []
        for j in ids:
            c0 = pl.multiple_of(j * tq, tq)
            blocks.append((_dot_nt(kc_ref[pl.ds(c0, tq), :], qs), ct_ref[j]))
        _flash_step(blocks, m_ref, l_ref, acc_ref)

    _loop_in_pairs(i, past_blocks)
    o_lat_t = (acc_ref[...] / l_ref[...]).astype(BF16)
    for h in range(C_HEADS):
        o_ref[:, h * C_V:(h + 1) * C_V] = _dot_tn(o_lat_t[:, h * tq:(h + 1) * tq], wuv_ref[h]).astype(BF16)


def _mla_prompt(qcat, kcat, w_uv, mix, layer, bp, lp, tq):
    nq = lp // tq
    kern = functools.partial(_mla_prompt_kernel, tq=tq, nblk=nq)
    return pl.pallas_call(
        kern,
        grid=(bp, nq),
        in_specs=[pl.BlockSpec((tq, 1024), lambda b, i: (b * nq + i, 0)),
                  pl.BlockSpec((lp, 256), lambda b, i: (b, 0)),
                  pl.BlockSpec((None, C_HEADS, C_KV_RANK, C_V), lambda b, i: (layer, 0, 0, 0)),
                  pl.BlockSpec(memory_space=pl.ANY)],
        out_specs=pl.BlockSpec((tq, C_WIDTH), lambda b, i: (b * nq + i, (A_WIDTH + B_WIDTH) // C_WIDTH)),
        out_shape=jax.ShapeDtypeStruct(mix.shape, BF16),
        scratch_shapes=[pltpu.VMEM((nq, C_KV_RANK, tq), BF16),
                        pltpu.VMEM((1, C_HEADS * tq), F32),
                        pltpu.VMEM((1, C_HEADS * tq), F32),
                        pltpu.VMEM((C_KV_RANK, C_HEADS * tq), F32)],
        input_output_aliases={3: 0},
        compiler_params=_cparams(("parallel", "arbitrary")),
        name="mla_prompt",
    )(qcat, kcat, w_uv, mix)


def _attn_sample_kernel(pt_ref, qm_ref, knew_ref, vnew_ref, qc_ref, kcnew_ref, wuv_ref,
                        ck_hbm, cv_hbm, cc_hbm, cr_hbm,
                        yb_ref, yc_ref,
                        kbuf, vbuf, cbuf, rbuf, sems, kmean_ref, sm_ref, sc_ref,
                        *, n_pages, page_off, ls):
    b = pl.program_id(0)
    nbatch = pl.num_programs(0)
    slot = lax.rem(b, 2)
    n_past = n_pages * PAGE_SIZE
    nblk = n_past // MOBA_BLOCK
    ppb = MOBA_BLOCK // PAGE_SIZE
    rows = B_HEADS * ls

    def page_copies(bb, p, sl):
        pg = pt_ref[bb, p] + page_off
        dst = pl.ds(pl.multiple_of(p * PAGE_SIZE, PAGE_SIZE), PAGE_SIZE)
        return (pltpu.make_async_copy(ck_hbm.at[pg], kbuf.at[sl, dst, :], sems.at[0, sl]),
                pltpu.make_async_copy(cv_hbm.at[pg], vbuf.at[sl, dst, :], sems.at[1, sl]),
                pltpu.make_async_copy(cc_hbm.at[pg], cbuf.at[sl, dst, :], sems.at[2, sl]),
                pltpu.make_async_copy(cr_hbm.at[pg], rbuf.at[sl, p], sems.at[3, sl]))

    unroll = 4 if n_pages % 4 == 0 else 1

    def start_batch(bb, sl):
        def body(p, carry):
            for cp in page_copies(bb, p, sl):
                cp.start()
            return carry
        lax.fori_loop(0, n_pages, body, 0, unroll=unroll)

    def wait_batch(bb, sl):
        def body(p, carry):
            for cp in page_copies(bb, p, sl):
                cp.wait()
            return carry
        lax.fori_loop(0, n_pages, body, 0, unroll=unroll)

    @pl.when(b == 0)
    def _():
        start_batch(b, slot)

    @pl.when(b + 1 < nbatch)
    def _():
        start_batch(b + 1, 1 - slot)

    wait_batch(b, slot)

    bpc = 4 if nblk % 4 == 0 else (2 if nblk % 2 == 0 else 1)
    chunk = bpc * MOBA_BLOCK
    nchunk = n_past // chunk

    rtok = lax.rem(lax.broadcasted_iota(jnp.int32, (rows, ls), 0), ls)
    ctok = lax.broadcasted_iota(jnp.int32, (rows, ls), 1)
    causal_new = ctok <= rtok

    qs = _stack_heads(qm_ref[...], B_HEADS, HEAD_DIM)
    qb = (qs * (HEAD_DIM ** -0.5)).astype(BF16)
    kmean_ref[...] = jnp.zeros_like(kmean_ref)
    for c in range(nchunk):
        kc = kbuf[slot, c * chunk:(c + 1) * chunk, :]
        for t in range(bpc):
            j = c * bpc + t
            kmean_ref[j:j + 1, :] = _block_mean(kc[t * MOBA_BLOCK:(t + 1) * MOBA_BLOCK, :])
        sm_ref[:, c * chunk:(c + 1) * chunk] = _dot_nt(qb, kc.astype(BF16))
    gate = _dot_nt(qs, kmean_ref[...], precision=HIGHEST)
    picks = _top_blocks(gate, nblk, 1)
    for j in range(nblk):
        chosen = (picks[0] == j) | (picks[1] == j) | (picks[2] == j)
        blk = slice(j * MOBA_BLOCK, (j + 1) * MOBA_BLOCK)
        sm_ref[:, blk] = jnp.where(chosen, sm_ref[:, blk], NEG)
    s_new = jnp.where(causal_new, _dot_nt(qb, knew_ref[...].astype(BF16)), NEG)
    s_all = sm_ref[...]
    m = jnp.maximum(jnp.max(s_all, axis=1, keepdims=True), jnp.max(s_new, axis=1, keepdims=True))
    p_new = jnp.exp(s_new - m)
    sm_ref[...] = jnp.exp(s_all - m)
    den = jnp.sum(sm_ref[...], axis=1, keepdims=True) + jnp.sum(p_new, axis=1, keepdims=True)
    acc = _dot(p_new.astype(BF16), vnew_ref[...].astype(BF16))
    for c in range(nchunk):
        vc = vbuf[slot, c * chunk:(c + 1) * chunk, :].astype(BF16)
        acc = acc + _dot(sm_ref[:, c * chunk:(c + 1) * chunk].astype(BF16), vc)
    yb_ref[...] = _unstack_heads(acc / den, B_HEADS, ls).astype(BF16)

    qc = _stack_heads(qc_ref[...], C_HEADS, 256)
    ql = qc[:, :C_KV_RANK]
    qr = qc[:, C_KV_RANK:C_KV_RANK + C_ROPE]
    ppc = chunk // PAGE_SIZE
    for c in range(nchunk):
        cc = cbuf[slot, c * chunk:(c + 1) * chunk, :].astype(BF16)
        rc = jnp.concatenate([rbuf[slot, c * ppc + t] for t in range(ppc)], axis=1).astype(BF16)
        sc_ref[:, c * chunk:(c + 1) * chunk] = _dot_nt(ql, cc) + _dot(qr, rc)
    kcn = kcnew_ref[...]
    s_new = jnp.where(causal_new, _dot_nt(qc, kcn), NEG)
    s_all = sc_ref[...]
    m = jnp.maximum(jnp.max(s_all, axis=1, keepdims=True), jnp.max(s_new, axis=1, keepdims=True))
    p_new = jnp.exp(s_new - m)
    sc_ref[...] = jnp.exp(s_all - m)
    den = jnp.sum(sc_ref[...], axis=1, keepdims=True) + jnp.sum(p_new, axis=1, keepdims=True)
    acc = _dot(p_new.astype(BF16), kcn[:, :C_KV_RANK])
    for c in range(nchunk):
        cc = cbuf[slot, c * chunk:(c + 1) * chunk, :].astype(BF16)
        acc = acc + _dot(sc_ref[:, c * chunk:(c + 1) * chunk].astype(BF16), cc)
    o_lat = (acc / den).astype(BF16)
    for h in range(C_HEADS):
        yc_ref[:, h * C_V:(h + 1) * C_V] = _dot(o_lat[h * ls:(h + 1) * ls, :], wuv_ref[h]).astype(BF16)


def _attn_sample(page_table, q_rot3, k_rot3, v3, qcat3, kcat3, w_uv, caches, layer, n_pool, bs, ls):
    n_pages = page_table.shape[1]
    n_past = n_pages * PAGE_SIZE
    nblk = n_past // MOBA_BLOCK
    nblk_pad = -(-nblk // 8) * 8
    rows = B_HEADS * ls
    kern = functools.partial(_attn_sample_kernel, n_pages=n_pages, page_off=layer * n_pool, ls=ls)
    new = lambda w: pl.BlockSpec((None, ls, w), lambda b, pt: (b, 0, 0))
    grid_spec = pltpu.PrefetchScalarGridSpec(
        num_scalar_prefetch=1,
        grid=(bs,),
        in_specs=[new(B_WIDTH), new(HEAD_DIM), new(HEAD_DIM), new(1024), new(256),
                  pl.BlockSpec((None, C_HEADS, C_KV_RANK, C_V), lambda b, pt: (layer, 0, 0, 0)),
                  pl.BlockSpec(memory_space=pl.ANY), pl.BlockSpec(memory_space=pl.ANY),
                  pl.BlockSpec(memory_space=pl.ANY), pl.BlockSpec(memory_space=pl.ANY)],
        out_specs=[new(B_WIDTH), new(C_WIDTH)],
        scratch_shapes=[pltpu.VMEM((2, n_past, HEAD_DIM), F32),
                        pltpu.VMEM((2, n_past, HEAD_DIM), F32),
                        pltpu.VMEM((2, n_past, C_KV_RANK), F32),
                        pltpu.VMEM((2, n_pages, C_ROPE, PAGE_SIZE), F32),
                        pltpu.SemaphoreType.DMA((4, 2)),
                        pltpu.VMEM((nblk_pad, HEAD_DIM), F32),
                        pltpu.VMEM((rows, n_past), F32),
                        pltpu.VMEM((rows, n_past), F32)])
    return pl.pallas_call(
        kern,
        grid_spec=grid_spec,
        out_shape=[jax.ShapeDtypeStruct((bs, ls, B_WIDTH), BF16),
                   jax.ShapeDtypeStruct((bs, ls, C_WIDTH), BF16)],
        compiler_params=_cparams(("arbitrary",)),
        name="attn_sample",
    )(page_table, q_rot3, k_rot3, v3, qcat3, kcat3, w_uv, *caches)


def _rope_tables(pos, d, reps):
    half = d // 2
    inv = ROPE_THETA ** (-jnp.arange(half, dtype=F32) * 2.0 / d)
    ang = pos.astype(F32)[:, None] * inv[None, :]
    cos = jnp.cos(ang)
    sin = jnp.sin(ang)
    return (jnp.tile(jnp.concatenate([cos, cos], axis=1), (1, reps)),
            jnp.tile(jnp.concatenate([-sin, sin], axis=1), (1, reps)))


def _lane_vec(v, lane0):
    depth, heads = v.shape
    out = jnp.zeros((depth, 1, 128), F32)
    return out.at[:, 0, lane0:lane0 + heads].set(v.astype(F32))


def kernel(x_prompt, x_sample, cache_moba_k, cache_moba_v, cache_mla_ckv, cache_mla_krope, state_delta, state_conv, page_table, w_in, w_conv, a_log, dt_bias, g_norm_a, g_q, w_uq, g_kv, w_uk, w_uv, w_o, ln1_g, ln1_b, w_ffn_in, w_ffn_out, ln2_g, ln2_b):
    bp, lp, d = x_prompt.shape
    bs, ls, _ = x_sample.shape
    depth = w_in.shape[0]
    n_pool = cache_moba_k.shape[1]
    n_pages = page_table.shape[1]
    n_past = n_pages * PAGE_SIZE
    n_p = bp * lp
    n_s = bs * ls
    n = n_p + n_s
    assert d == D_MODEL and lp % MOBA_BLOCK == 0 and n_past % MOBA_BLOCK == 0 and ls <= 8
    alpha = (2 * depth) ** 0.25

    tm = math.gcd(math.gcd(n_p, n_s), 512)
    tile_a = 256 if lp % 256 == 0 else DELTA_SUB
    tm_proj = n // 8 if n % 128 == 0 else tm

    qa_end = 4 * A_WIDTH
    o_ba = qa_end
    o_qb = o_ba + 2 * A_HEADS
    o_kb = o_qb + B_WIDTH
    o_vb = o_kb + HEAD_DIM
    o_cq = o_vb + HEAD_DIM
    o_ckv = o_cq + C_Q_RANK
    o_kr = o_ckv + C_KV_RANK
    o_end = o_kr + C_ROPE
    used = OFF_KR + C_ROPE + 2 * A_HEADS
    w_in_p = jnp.concatenate(
        [w_in[..., :qa_end], w_in[..., o_qb:o_kb], w_in[..., o_cq:o_ckv], w_in[..., o_kb:o_vb],
         w_in[..., o_vb:o_cq], w_in[..., o_ckv:o_kr], w_in[..., o_kr:o_end], w_in[..., o_ba:o_qb],
         jnp.zeros(w_in.shape[:2] + (H_PAD - used,), w_in.dtype)], axis=-1).astype(BF16)
    w_uq_p = jnp.concatenate([w_uq[..., :C_NOPE].reshape(depth, C_Q_RANK, C_HEADS * C_NOPE),
                              w_uq[..., C_NOPE:].reshape(depth, C_Q_RANK, C_HEADS * C_ROPE)], axis=-1).astype(BF16)
    w_ukt = jnp.transpose(w_uk, (0, 2, 3, 1)).astype(BF16)
    w_uv_p = jnp.transpose(w_uv, (0, 2, 1, 3)).astype(BF16)
    w_o_b = w_o.astype(BF16)
    w_ffn_in_b = w_ffn_in.astype(BF16)
    w_ffn_out_b = w_ffn_out.astype(BF16)
    alog_vec = _lane_vec(a_log, LANE_DECAY)
    dtb_vec = _lane_vec(dt_bias, LANE_DECAY)
    g_norm3 = g_norm_a.reshape(depth, 1, HEAD_DIM)
    g_q3 = g_q.reshape(depth, 1, C_Q_RANK)
    g_kv3 = g_kv.reshape(depth, 1, C_KV_RANK)
    ln1_g3, ln1_b3 = ln1_g.reshape(depth, 1, d), ln1_b.reshape(depth, 1, d)
    ln2_g3, ln2_b3 = ln2_g.reshape(depth, 1, d), ln2_b.reshape(depth, 1, d)

    pos = jnp.concatenate([jnp.tile(jnp.arange(lp, dtype=jnp.int32), bp),
                           jnp.tile(n_past + jnp.arange(ls, dtype=jnp.int32), bs)])
    tabs = _rope_tables(pos, HEAD_DIM, 1) + _rope_tables(pos, C_ROPE, 4)

    caches = (cache_moba_k.reshape(depth * n_pool, PAGE_SIZE, HEAD_DIM),
              cache_moba_v.reshape(depth * n_pool, PAGE_SIZE, HEAD_DIM),
              cache_mla_ckv.reshape(depth * n_pool, PAGE_SIZE, C_KV_RANK),
              jnp.swapaxes(cache_mla_krope, 2, 3).reshape(depth * n_pool, C_ROPE, PAGE_SIZE))

    x = jnp.concatenate([x_prompt.reshape(n_p, d), x_sample.reshape(n_s, d)], axis=0)
    xb = x.astype(BF16)
    mix = jnp.zeros((n, MIX_WIDTH), BF16)
    s_all = jnp.zeros(state_delta.shape, F32)
    rec = []
    for layer in range(depth):
        h = _matmul(xb, w_in_p, layer, tm_proj, H_PAD // 4)
        q_rot, k_rot, ckv_n, kr_rot, qcat, kcat = _prep(h, tabs, g_q3, g_kv3, w_uq_p, w_ukt, layer, tm)
        vb = h[:, OFF_VB:OFF_VB + HEAD_DIM]

        mix, s_p, conv_p = _delta_prompt(h, mix, w_conv, alog_vec, dtb_vec, g_norm3, layer, bp, lp, tile_a)
        mix = _moba_prompt(q_rot, k_rot, vb, mix, bp, lp)
        mix = _mla_prompt(qcat, kcat, w_uv_p, mix, layer, bp, lp, MOBA_BLOCK)

        samp = lambda a: a[n_p:].reshape(bs, ls, a.shape[-1])
        ya_s, s_all, conv_s = _delta_sample(samp(h), state_conv, state_delta, s_all, w_conv, alog_vec, dtb_vec,
                                            g_norm3, layer, bs, ls)
        yb_s, yc_s = _attn_sample(page_table, samp(q_rot), samp(k_rot), samp(vb), samp(qcat), samp(kcat),
                                  w_uv_p, caches, layer, n_pool, bs, ls)
        mix_s = jnp.concatenate([ya_s.reshape(n_s, A_WIDTH), yb_s.reshape(n_s, B_WIDTH), yc_s.reshape(n_s, C_WIDTH)],
                                axis=1)
        mix = lax.dynamic_update_slice(mix, mix_s, (n_p, 0))

        x1, x1b = _outproj_ln(mix, w_o_b, x, ln1_g3, ln1_b3, layer, alpha, tm)
        x, xb = _ffn_ln(x1b, x1, w_ffn_in_b, w_ffn_out_b, ln2_g3, ln2_b3, layer, alpha, tm, 512)
        rec.append((k_rot, vb, ckv_n, kr_rot, s_p, conv_p, conv_s))

    def stack(idx, lo, hi, shape):
        return jnp.stack([r[idx][lo:hi].reshape(shape) for r in rec])

    return (x[:n_p].reshape(bp, lp, d), x[n_p:].reshape(bs, ls, d),
            stack(0, 0, n_p, (bp, lp, 1, HEAD_DIM)), stack(1, 0, n_p, (bp, lp, 1, HEAD_DIM)),
            stack(2, 0, n_p, (bp, lp, C_KV_RANK)), stack(3, 0, n_p, (bp, lp, C_ROPE)),
            jnp.stack([r[4] for r in rec]), jnp.stack([r[5] for r in rec]),
            stack(0, n_p, n, (bs, ls, 1, HEAD_DIM)), stack(1, n_p, n, (bs, ls, 1, HEAD_DIM)),
            stack(2, n_p, n, (bs, ls, C_KV_RANK)), stack(3, n_p, n, (bs, ls, C_ROPE)),
            s_all, jnp.stack([r[6] for r in rec]))
```

```python
import functools
import math

import jax
import jax.numpy as jnp
from jax import lax
from jax.experimental import pallas as pl
from jax.experimental.pallas import tpu as pltpu

F32 = jnp.float32
BF16 = jnp.bfloat16
HIGHEST = lax.Precision.HIGHEST

D_MODEL = 2048
PAGE_SIZE = 128
HEAD_DIM = 128
A_HEADS = 8
A_WIDTH = A_HEADS * HEAD_DIM
CONV_W = 4
DELTA_CHUNK = 64
B_HEADS = 4
B_WIDTH = B_HEADS * HEAD_DIM
MOBA_BLOCK = 256
MOBA_TOPK = 3
C_HEADS = 4
C_NOPE = 128
C_ROPE = 64
C_V = 128
C_Q_RANK = 384
C_KV_RANK = 128
C_WIDTH = C_HEADS * C_V
MIX_WIDTH = A_WIDTH + B_WIDTH + C_WIDTH
ROPE_THETA = 10000.0
LN_EPS = 1e-5
RMS_EPS = 1e-6
L2_EPS = 1e-6

OFF_QKV = 0
OFF_Z = 3 * A_WIDTH
HA_WIDTH = 4 * A_WIDTH
OFF_QB = 0
OFF_KB = OFF_QB + B_WIDTH
OFF_VB = OFF_KB + HEAD_DIM
OFF_CQ = OFF_VB + HEAD_DIM
OFF_CKV = OFF_CQ + C_Q_RANK
OFF_KR = OFF_CKV + C_KV_RANK
HB_WIDTH = OFF_KR + 128
LANE_BETA = C_ROPE
LANE_DECAY = C_ROPE + A_HEADS

NEG = -1e30
VMEM_LIMIT = 56 * 1024 * 1024
DELTA_SUB = 128


def _cparams(sem):
    return pltpu.CompilerParams(dimension_semantics=sem, vmem_limit_bytes=VMEM_LIMIT)


def _dot(a, b):
    return jnp.dot(a, b, preferred_element_type=F32)


def _dot_exact(a, b):
    return jnp.dot(a, b, preferred_element_type=F32, precision=HIGHEST)


def _dot_nt(a, b, precision=None):
    return lax.dot_general(a, b, (((1,), (1,)), ((), ())), preferred_element_type=F32, precision=precision)


def _dot_tn(a, b, precision=None):
    return lax.dot_general(a, b, (((0,), (0,)), ((), ())), preferred_element_type=F32, precision=precision)


def _dot_b(a, b):
    return _dot(a.astype(BF16), b.astype(BF16))


def _matmul_kernel(x_ref, w_ref, o_ref):
    o_ref[...] = _dot(x_ref[...], w_ref[...])


def _matmul(xb, w, layer, tm, tn):
    m, k = xb.shape
    n = w.shape[-1]
    return pl.pallas_call(
        _matmul_kernel,
        grid=(m // tm, n // tn),
        in_specs=[pl.BlockSpec((tm, k), lambda i, j: (i, 0)),
                  pl.BlockSpec((None, k, tn), lambda i, j: (layer, 0, j))],
        out_specs=pl.BlockSpec((tm, tn), lambda i, j: (i, j)),
        out_shape=jax.ShapeDtypeStruct((m, n), F32),
        compiler_params=_cparams(("parallel", "arbitrary")),
        name="proj_in",
    )(xb, w)


def _layer_norm_rows(y, g, b):
    mu = jnp.mean(y, axis=-1, keepdims=True)
    yc = y - mu
    var = jnp.mean(yc * yc, axis=-1, keepdims=True)
    return yc * lax.rsqrt(var + LN_EPS) * g + b


def _outproj_ln_kernel(mix_ref, w_ref, x_ref, g_ref, b_ref, o_ref, ob_ref, *, alpha):
    y = alpha * x_ref[...] + _dot(mix_ref[...], w_ref[...])
    out = _layer_norm_rows(y, g_ref[...], b_ref[...])
    o_ref[...] = out
    ob_ref[...] = out.astype(BF16)


def _outproj_ln(mix, w_o, x, g, b, layer, alpha, tm):
    m, d = x.shape
    kdim = mix.shape[1]
    return pl.pallas_call(
        functools.partial(_outproj_ln_kernel, alpha=alpha),
        grid=(m // tm,),
        in_specs=[pl.BlockSpec((tm, kdim), lambda i: (i, 0)),
                  pl.BlockSpec((None, kdim, d), lambda i: (layer, 0, 0)),
                  pl.BlockSpec((tm, d), lambda i: (i, 0)),
                  pl.BlockSpec((None, 1, d), lambda i: (layer, 0, 0)),
                  pl.BlockSpec((None, 1, d), lambda i: (layer, 0, 0))],
        out_specs=[pl.BlockSpec((tm, d), lambda i: (i, 0)),
                   pl.BlockSpec((tm, d), lambda i: (i, 0))],
        out_shape=[jax.ShapeDtypeStruct((m, d), F32), jax.ShapeDtypeStruct((m, d), BF16)],
        compiler_params=_cparams(("parallel",)),
        name="outproj_ln",
    )(mix, w_o, x, g, b)


def _ffn_ln_kernel(xb_ref, wg_ref, wu_ref, wo_ref, x_ref, g_ref, b_ref, o_ref, ob_ref, acc_ref, *, alpha):
    f = pl.program_id(1)

    @pl.when(f == 0)
    def _():
        acc_ref[...] = jnp.zeros_like(acc_ref)

    xb = xb_ref[...]
    gate = _dot(xb, wg_ref[...])
    up = _dot(xb, wu_ref[...])
    act = (jax.nn.silu(gate) * up).astype(BF16)
    acc_ref[...] += _dot(act, wo_ref[...])

    @pl.when(f == pl.num_programs(1) - 1)
    def _():
        y = alpha * x_ref[...] + acc_ref[...]
        out = _layer_norm_rows(y, g_ref[...], b_ref[...])
        o_ref[...] = out
        ob_ref[...] = out.astype(BF16)


def _ffn_ln(xb, x, w_in, w_out, g, b, layer, alpha, tm, tf):
    m, d = x.shape
    d_ff = w_out.shape[1]
    nf = d_ff // tf
    return pl.pallas_call(
        functools.partial(_ffn_ln_kernel, alpha=alpha),
        grid=(m // tm, nf),
        in_specs=[pl.BlockSpec((tm, d), lambda i, f: (i, 0)),
                  pl.BlockSpec((None, d, tf), lambda i, f: (layer, 0, f)),
                  pl.BlockSpec((None, d, tf), lambda i, f: (layer, 0, nf + f)),
                  pl.BlockSpec((None, tf, d), lambda i, f: (layer, f, 0)),
                  pl.BlockSpec((tm, d), lambda i, f: (i, 0)),
                  pl.BlockSpec((None, 1, d), lambda i, f: (layer, 0, 0)),
                  pl.BlockSpec((None, 1, d), lambda i, f: (layer, 0, 0))],
        out_specs=[pl.BlockSpec((tm, d), lambda i, f: (i, 0)),
                   pl.BlockSpec((tm, d), lambda i, f: (i, 0))],
        out_shape=[jax.ShapeDtypeStruct((m, d), F32), jax.ShapeDtypeStruct((m, d), BF16)],
        scratch_shapes=[pltpu.VMEM((tm, d), F32)],
        compiler_params=_cparams(("parallel", "arbitrary")),
        name="ffn_ln",
    )(xb, w_in, w_in, w_out, x, g, b)


def _swap_half64(a):
    lane = lax.broadcasted_iota(jnp.int32, a.shape, 1)
    first = jnp.bitwise_and(lane, 63) < 32
    return jnp.where(first, pltpu.roll(a, 96, 1), pltpu.roll(a, 32, 1))


def _prep_kernel(qb_ref, cq_ref, kb_ref, ckv_ref, kr_ref, cos_ref, sin_ref, cos64_ref, sin64_ref,
                 gq_ref, gkv_ref, wuq_ref, wukt_ref,
                 qrot_ref, krot_ref, ckvn_ref, krr_ref, qcat_ref, kcat_ref):
    cos = cos_ref[...]
    sin = sin_ref[...]
    q = qb_ref[...]
    for h in range(B_HEADS):
        xs = q[:, h * HEAD_DIM:(h + 1) * HEAD_DIM]
        qrot_ref[:, h * HEAD_DIM:(h + 1) * HEAD_DIM] = xs * cos + pltpu.roll(xs, HEAD_DIM // 2, 1) * sin
    k = kb_ref[...]
    krot_ref[...] = k * cos + pltpu.roll(k, HEAD_DIM // 2, 1) * sin

    cq = cq_ref[...]
    cqn = cq * lax.rsqrt(jnp.mean(cq * cq, axis=-1, keepdims=True) + RMS_EPS) * gq_ref[...]
    qfull = _dot(cqn.astype(BF16), wuq_ref[...])
    c64 = cos64_ref[...]
    s64 = sin64_ref[...]
    nope_w = C_HEADS * C_NOPE
    halves = []
    for half in range(2):
        a = qfull[:, nope_w + half * 128: nope_w + (half + 1) * 128]
        halves.append(a * c64[:, half * 128:(half + 1) * 128] + _swap_half64(a) * s64[:, half * 128:(half + 1) * 128])
    scale = (C_NOPE + C_ROPE) ** -0.5
    zpad = jnp.zeros((q.shape[0], 64), F32)
    for h in range(C_HEADS):
        ql = _dot(qfull[:, h * C_NOPE:(h + 1) * C_NOPE].astype(BF16), wukt_ref[h])
        hr = halves[h // 2][:, (h % 2) * C_ROPE:(h % 2 + 1) * C_ROPE]
        qcat_ref[:, h * 256:(h + 1) * 256] = (jnp.concatenate([ql, hr, zpad], axis=1) * scale).astype(BF16)

    ck = ckv_ref[...]
    ckn = ck * lax.rsqrt(jnp.mean(ck * ck, axis=-1, keepdims=True) + RMS_EPS) * gkv_ref[...]
    ckvn_ref[...] = ckn
    krb = kr_ref[...]
    krr = (krb * c64[:, :128] + _swap_half64(krb) * s64[:, :128])[:, :C_ROPE]
    krr_ref[...] = krr
    kcat_ref[...] = jnp.concatenate([ckn, krr, zpad], axis=1).astype(BF16)


def _prep(h, tabs, g_q, g_kv, w_uq, w_ukt, layer, tm):
    n = h.shape[0]
    cos128, sin128, cos64, sin64 = tabs
    row = lambda w, off: pl.BlockSpec((tm, w), lambda i: (i, off // w))
    tab = lambda w: pl.BlockSpec((tm, w), lambda i: (i, 0))
    return pl.pallas_call(
        _prep_kernel,
        grid=(n // tm,),
        in_specs=[row(B_WIDTH, OFF_QB), row(C_Q_RANK, OFF_CQ), row(HEAD_DIM, OFF_KB),
                  row(C_KV_RANK, OFF_CKV), row(128, OFF_KR),
                  tab(128), tab(128), tab(256), tab(256),
                  pl.BlockSpec((None, 1, C_Q_RANK), lambda i: (layer, 0, 0)),
                  pl.BlockSpec((None, 1, C_KV_RANK), lambda i: (layer, 0, 0)),
                  pl.BlockSpec((None, C_Q_RANK, 768), lambda i: (layer, 0, 0)),
                  pl.BlockSpec((None, C_HEADS, C_NOPE, C_KV_RANK), lambda i: (layer, 0, 0, 0))],
        out_specs=[tab(B_WIDTH), tab(HEAD_DIM), tab(C_KV_RANK), tab(C_ROPE), tab(1024), tab(256)],
        out_shape=[jax.ShapeDtypeStruct((n, B_WIDTH), F32),
                   jax.ShapeDtypeStruct((n, HEAD_DIM), F32),
                   jax.ShapeDtypeStruct((n, C_KV_RANK), F32),
                   jax.ShapeDtypeStruct((n, C_ROPE), F32),
                   jax.ShapeDtypeStruct((n, 1024), BF16),
                   jax.ShapeDtypeStruct((n, 256), BF16)],
        compiler_params=_cparams(("parallel",)),
        name="prep",
    )(h, h, h, h, h, cos128, sin128, cos64, sin64, g_q, g_kv, w_uq, w_ukt)


def _inv_unit_lower(mats, chunk):
    n = mats[0].shape[0]
    r = lax.broadcasted_iota(jnp.int32, (n, n), 0)
    col = lax.broadcasted_iota(jnp.int32, (n, n), 1)
    eye = (r == col).astype(F32)
    blk8 = jnp.right_shift(r, 3) == jnp.right_shift(col, 3)
    a8f = [jnp.where(blk8, a, 0.0) for a in mats]
    a8 = [a.astype(BF16) for a in a8f]
    xs = [eye - a for a in a8f]
    ps = [_dot(a, a) for a in a8]
    xs = [x + _dot_b(x, p) for x, p in zip(xs, ps)]
    ps = [_dot_b(p, p) for p in ps]
    xs = [x + _dot_b(x, p) for x, p in zip(xs, ps)]
    k = 8
    while k < chunk:
        sh = k.bit_length() - 1
        same2k = jnp.right_shift(r, sh + 1) == jnp.right_shift(col, sh + 1)
        samek = jnp.right_shift(r, sh) == jnp.right_shift(col, sh)
        off = same2k & jnp.logical_not(samek)
        ms = [jnp.where(off, a, 0.0).astype(BF16) for a in mats]
        xb = [x.astype(BF16) for x in xs]
        ts = [_dot(x, m) for x, m in zip(xb, ms)]
        xs = [x - _dot(t.astype(BF16), xh) for x, t, xh in zip(xs, ts, xb)]
        k *= 2
    return xs


def _wy_prepare(systems, chunk):
    n = systems[0][0].shape[0]
    r = lax.broadcasted_iota(jnp.int32, (n, n), 0)
    col = lax.broadcasted_iota(jnp.int32, (n, n), 1)
    sh = chunk.bit_length() - 1
    incl = (jnp.right_shift(r, sh) == jnp.right_shift(col, sh)) & (r >= col)
    strict = r > col
    kbs = [s[1].astype(BF16) for s in systems]
    kks = [_dot_nt(kb, kb) for kb in kbs]
    qks = [_dot_nt(s[0].astype(BF16), kb) for s, kb in zip(systems, kbs)]
    decays = [jnp.exp(jnp.where(incl, s[3] - s[4], -jnp.inf)) for s in systems]
    mats = [jnp.where(strict, s[5] * kk * dc, 0.0) for s, kk, dc in zip(systems, kks, decays)]
    tinvs = _inv_unit_lower(mats, chunk)
    egs = [jnp.exp(s[3]) for s in systems]
    rhs = [jnp.concatenate([s[5] * s[2], (s[5] * eg) * s[1]], axis=1) for s, eg in zip(systems, egs)]
    sols = [_dot_b(t, x) for t, x in zip(tinvs, rhs)]
    return [(sol[:, :HEAD_DIM], sol[:, HEAD_DIM:], s[0] * eg, qk * dc)
            for sol, s, eg, qk, dc in zip(sols, systems, egs, qks, decays)]


def _l2norm_rows(x):
    return x * lax.rsqrt(jnp.sum(x * x, axis=-1, keepdims=True) + L2_EPS)


def _decay_and_beta(bb, alog_ref, dtb_ref, valid):
    lane = lax.broadcasted_iota(jnp.int32, bb.shape, 1)
    is_decay = (lane >= LANE_DECAY) & (lane < LANE_DECAY + A_HEADS)
    g = -jnp.exp(alog_ref[...]) * jax.nn.softplus(bb + dtb_ref[...])
    g = jnp.where(is_decay & valid, g, 0.0)
    beta = jnp.where(valid, jax.nn.sigmoid(bb), 0.0)
    return g, beta


def _gated_out(o, z, gnorm):
    on = o * lax.rsqrt(jnp.mean(o * o, axis=-1, keepdims=True) + RMS_EPS) * gnorm
    return (on * jax.nn.silu(z)).astype(BF16)


def _delta_prompt_kernel(u_ref, z_ref, ba_ref, wconv_ref, alog_ref, dtb_ref, gnorm_ref, mix_hbm,
                         ya_ref, sfin_ref, cfin_ref,
                         ubuf, s_ref, qn_ref, kn_ref, vn_ref, u_s, wq_s, qkd_s, o_s, d_s, *, tile, chunk):
    del mix_hbm
    t = pl.program_id(1)
    nchunk = tile // chunk
    nsub = tile // DELTA_SUB

    @pl.when(t == 0)
    def _():
        ubuf[0:8, :] = jnp.zeros((8, ubuf.shape[1]), F32)
        s_ref[...] = jnp.zeros_like(s_ref)

    ubuf[8:8 + tile, :] = u_ref[...]
    for grp in range(3 * A_HEADS):
        lo = grp * HEAD_DIM
        y = ubuf[5:5 + tile, lo:lo + HEAD_DIM] * wconv_ref[0:1, lo:lo + HEAD_DIM]
        for i in range(1, CONV_W):
            y = y + ubuf[5 + i:5 + i + tile, lo:lo + HEAD_DIM] * wconv_ref[i:i + 1, lo:lo + HEAD_DIM]
        y = jax.nn.silu(y)
        hh = grp % A_HEADS
        if grp < A_HEADS:
            qn_ref[hh] = _l2norm_rows(y) * (HEAD_DIM ** -0.5)
        elif grp < 2 * A_HEADS:
            kn_ref[hh] = _l2norm_rows(y)
        else:
            vn_ref[hh] = y
    cfin_ref[...] = ubuf[tile + 5:tile + 8, :]
    ubuf[0:8, :] = ubuf[tile:tile + 8, :]

    g, beta = _decay_and_beta(ba_ref[...], alog_ref, dtb_ref, True)
    r = lax.broadcasted_iota(jnp.int32, (tile, tile), 0)
    col = lax.broadcasted_iota(jnp.int32, (tile, tile), 1)
    sh = chunk.bit_length() - 1
    same = jnp.right_shift(r, sh) == jnp.right_shift(col, sh)
    g_cum = _dot_exact(jnp.where(same & (r >= col), 1.0, 0.0), g)
    gt_cum = _dot_exact(g.T, jnp.where(same & (r <= col), 1.0, 0.0))

    where = [(hh, sb * DELTA_SUB) for hh in range(A_HEADS) for sb in range(nsub)]
    systems = []
    for hh, lo in where:
        hi = lo + DELTA_SUB
        ld = LANE_DECAY + hh
        lb = LANE_BETA + hh
        systems.append((qn_ref[hh, lo:hi, :], kn_ref[hh, lo:hi, :], vn_ref[hh, lo:hi, :],
                        g_cum[lo:hi, ld:ld + 1], gt_cum[ld:ld + 1, lo:hi], beta[lo:hi, lb:lb + 1]))
    for (hh, lo), (u, w, qe, qkd) in zip(where, _wy_prepare(systems, chunk)):
        u_s[hh, lo:lo + DELTA_SUB, :] = u
        for c in range(DELTA_SUB // chunk):
            a0 = lo + c * chunk
            wq_s[hh, 2 * a0:2 * a0 + chunk, :] = w[c * chunk:(c + 1) * chunk, :].astype(BF16)
            wq_s[hh, 2 * a0 + chunk:2 * a0 + 2 * chunk, :] = qe[c * chunk:(c + 1) * chunk, :].astype(BF16)
        qkd_s[hh, lo:lo + DELTA_SUB, :] = qkd.astype(BF16)

    for c in range(nchunk):
        lo, hi = c * chunk, (c + 1) * chunk
        for hh in range(A_HEADS):
            ld = LANE_DECAY + hh
            s = s_ref[hh]
            ws = _dot(wq_s[hh, 2 * lo:2 * hi, :], s.astype(BF16))
            delta = (u_s[hh, lo:hi, :] - ws[:chunk, :]).astype(BF16)
            o_s[hh, lo:hi, :] = ws[chunk:, :]
            d_s[hh, lo:hi, :] = delta
            g_last = g_cum[hi - 1:hi, ld:ld + 1]
            kd = kn_ref[hh, lo:hi, :] * jnp.exp(g_last - g_cum[lo:hi, ld:ld + 1])
            s_ref[hh] = s * jnp.exp(g_last) + _dot_tn(kd.astype(BF16), delta)

    gnorm = gnorm_ref[...]
    for hh in range(A_HEADS):
        for sb in range(nsub):
            lo, hi = sb * DELTA_SUB, (sb + 1) * DELTA_SUB
            o = o_s[hh, lo:hi, :] + _dot(qkd_s[hh, lo:hi, :], d_s[hh, lo:hi, :])
            ya_ref[lo:hi, hh * HEAD_DIM:(hh + 1) * HEAD_DIM] = _gated_out(
                o, z_ref[lo:hi, hh * HEAD_DIM:(hh + 1) * HEAD_DIM], gnorm)

    @pl.when(t == pl.num_programs(1) - 1)
    def _():
        sfin_ref[...] = s_ref[...]


def _delta_prompt(ha, hb, mix, w_conv, alog_vec, dtb_vec, g_norm, layer, bp, lp, tile):
    nt = lp // tile
    qkv_w = 3 * A_WIDTH
    kern = functools.partial(_delta_prompt_kernel, tile=tile, chunk=DELTA_CHUNK)
    head_buf = lambda rows, dt: pltpu.VMEM((A_HEADS, rows, HEAD_DIM), dt)
    return pl.pallas_call(
        kern,
        grid=(bp, nt),
        in_specs=[pl.BlockSpec((tile, qkv_w), lambda b, t: (b * nt + t, 0)),
                  pl.BlockSpec((tile, A_WIDTH), lambda b, t: (b * nt + t, OFF_Z // A_WIDTH)),
                  pl.BlockSpec((tile, 128), lambda b, t: (b * nt + t, OFF_KR // 128)),
                  pl.BlockSpec((None, CONV_W, qkv_w), lambda b, t: (layer, 0, 0)),
                  pl.BlockSpec((None, 1, 128), lambda b, t: (layer, 0, 0)),
                  pl.BlockSpec((None, 1, 128), lambda b, t: (layer, 0, 0)),
                  pl.BlockSpec((None, 1, HEAD_DIM), lambda b, t: (layer, 0, 0)),
                  pl.BlockSpec(memory_space=pl.ANY)],
        out_specs=[pl.BlockSpec((tile, A_WIDTH), lambda b, t: (b * nt + t, 0)),
                   pl.BlockSpec((None, A_HEADS, HEAD_DIM, HEAD_DIM), lambda b, t: (b, 0, 0, 0)),
                   pl.BlockSpec((None, CONV_W - 1, qkv_w), lambda b, t: (b, 0, 0))],
        out_shape=[jax.ShapeDtypeStruct(mix.shape, BF16),
                   jax.ShapeDtypeStruct((bp, A_HEADS, HEAD_DIM, HEAD_DIM), F32),
                   jax.ShapeDtypeStruct((bp, CONV_W - 1, qkv_w), F32)],
        scratch_shapes=[pltpu.VMEM((tile + 8, qkv_w), F32),
                        head_buf(HEAD_DIM, F32),
                        head_buf(tile, F32), head_buf(tile, F32), head_buf(tile, F32),
                        head_buf(tile, F32), head_buf(2 * tile, BF16), head_buf(tile, BF16),
                        head_buf(tile, F32), head_buf(tile, BF16)],
        input_output_aliases={7: 0},
        compiler_params=_cparams(("parallel", "arbitrary")),
        name="delta_prompt",
    )(ha, ha, hb, w_conv, alog_vec, dtb_vec, g_norm, mix)


def _delta_sample_kernel(u_ref, z_ref, ba_ref, cs_ref, s0_ref, wconv_ref, alog_ref, dtb_ref, gnorm_ref, sall_hbm,
                         ya_ref, snew_ref, cnew_ref, buf, *, ls, group):
    del sall_hbm
    rows = 8
    width = buf.shape[2]
    heads = range(A_HEADS)
    rid = lax.broadcasted_iota(jnp.int32, (rows, 128), 0)
    r = lax.broadcasted_iota(jnp.int32, (rows, rows), 0)
    col = lax.broadcasted_iota(jnp.int32, (rows, rows), 1)
    tril = jnp.where(r >= col, 1.0, 0.0)

    def conv_group(gi, grp):
        lo = grp * HEAD_DIM
        y = buf[gi, 0:rows, lo:lo + HEAD_DIM] * wconv_ref[0:1, lo:lo + HEAD_DIM]
        for i in range(1, CONV_W):
            y = y + buf[gi, i:i + rows, lo:lo + HEAD_DIM] * wconv_ref[i:i + 1, lo:lo + HEAD_DIM]
        return jax.nn.silu(y)

    systems = []
    for gi in range(group):
        buf[gi, 0:CONV_W - 1, :] = cs_ref[gi]
        buf[gi, CONV_W - 1:CONV_W - 1 + ls, :] = u_ref[gi]
        buf[gi, CONV_W - 1 + ls:, :] = jnp.zeros((buf.shape[1] - (CONV_W - 1 + ls), width), F32)
        cnew_ref[gi] = buf[gi, ls:ls + CONV_W - 1, :]
        bb = jnp.concatenate([ba_ref[gi], jnp.zeros((rows - ls, 128), F32)], axis=0)
        g, beta = _decay_and_beta(bb, alog_ref, dtb_ref, rid < ls)
        g_cum = _dot_exact(tril, g)
        q = jnp.concatenate([_l2norm_rows(conv_group(gi, hh)) * (HEAD_DIM ** -0.5) for hh in heads], axis=0)
        k = jnp.concatenate([_l2norm_rows(conv_group(gi, A_HEADS + hh)) for hh in heads], axis=0)
        v = jnp.concatenate([conv_group(gi, 2 * A_HEADS + hh) for hh in heads], axis=0)
        g_col = jnp.concatenate([g_cum[:, LANE_DECAY + hh:LANE_DECAY + hh + 1] for hh in heads], axis=0)
        beta_col = jnp.concatenate([beta[:, LANE_BETA + hh:LANE_BETA + hh + 1] for hh in heads], axis=0)
        g_row = jnp.broadcast_to(g_col, (A_HEADS * rows, 128)).T[0:1, :]
        systems.append((q, k, v, g_col, g_row, beta_col))
    prepared = _wy_prepare(systems, rows)

    nrow = A_HEADS * rows
    wide = A_HEADS * HEAD_DIM
    row_head = jnp.bitwise_and(jnp.right_shift(lax.broadcasted_iota(jnp.int32, (2 * nrow, wide), 0), 3), A_HEADS - 1)
    col_head = jnp.right_shift(lax.broadcasted_iota(jnp.int32, (2 * nrow, wide), 1), 7)
    own_head = row_head == col_head
    gnorm = gnorm_ref[...]
    for gi in range(group):
        u, w, qe, qkd = prepared[gi]
        k, g_col = systems[gi][1], systems[gi][3]
        s_stack = s0_ref[gi].reshape(wide, HEAD_DIM)
        lhs = jnp.tile(jnp.concatenate([w, qe], axis=0), (1, A_HEADS))
        ws = _dot(jnp.where(own_head, lhs, 0.0).astype(BF16), s_stack.astype(BF16))
        delta = u - ws[:nrow, :]
        o = ws[nrow:, :] + _dot_b(qkd, delta)
        g_last = jnp.concatenate([jnp.broadcast_to(g_col[(hh + 1) * rows - 1:(hh + 1) * rows, :], (rows, 1))
                                  for hh in heads], axis=0)
        kd = k * jnp.exp(g_last - g_col)
        d_wide = jnp.where(own_head[:nrow, :], jnp.tile(delta, (1, A_HEADS)), 0.0)
        upd = _dot_tn(kd.astype(BF16), d_wide.astype(BF16))
        for hh in heads:
            lo = hh * HEAD_DIM
            decay_h = jnp.exp(g_col[(hh + 1) * rows - 1:(hh + 1) * rows, :])
            snew_ref[gi, hh] = s0_ref[gi, hh] * decay_h + upd[:, lo:lo + HEAD_DIM]
            ya_ref[gi, :, lo:lo + HEAD_DIM] = _gated_out(o[hh * rows:hh * rows + ls, :],
                                                          z_ref[gi, :, lo:lo + HEAD_DIM], gnorm)


def _delta_sample(ha3, hb3, state_conv, state_delta, s_all, w_conv, alog_vec, dtb_vec, g_norm, layer, bs, ls):
    qkv_w = 3 * A_WIDTH
    group = 4 if bs % 4 == 0 else 1
    kern = functools.partial(_delta_sample_kernel, ls=ls, group=group)
    return pl.pallas_call(
        kern,
        grid=(bs // group,),
        in_specs=[pl.BlockSpec((group, ls, qkv_w), lambda b: (b, 0, 0)),
                  pl.BlockSpec((group, ls, A_WIDTH), lambda b: (b, 0, OFF_Z // A_WIDTH)),
                  pl.BlockSpec((group, ls, 128), lambda b: (b, 0, OFF_KR // 128)),
                  pl.BlockSpec((None, group, CONV_W - 1, qkv_w), lambda b: (layer, b, 0, 0)),
                  pl.BlockSpec((None, group, A_HEADS, HEAD_DIM, HEAD_DIM), lambda b: (layer, b, 0, 0, 0)),
                  pl.BlockSpec((None, CONV_W, qkv_w), lambda b: (layer, 0, 0)),
                  pl.BlockSpec((None, 1, 128), lambda b: (layer, 0, 0)),
                  pl.BlockSpec((None, 1, 128), lambda b: (layer, 0, 0)),
                  pl.BlockSpec((None, 1, HEAD_DIM), lambda b: (layer, 0, 0)),
                  pl.BlockSpec(memory_space=pl.ANY)],
        out_specs=[pl.BlockSpec((group, ls, A_WIDTH), lambda b: (b, 0, 0)),
                   pl.BlockSpec((None, group, A_HEADS, HEAD_DIM, HEAD_DIM), lambda b: (layer, b, 0, 0, 0)),
                   pl.BlockSpec((group, CONV_W - 1, qkv_w), lambda b: (b, 0, 0))],
        out_shape=[jax.ShapeDtypeStruct((bs, ls, A_WIDTH), BF16),
                   jax.ShapeDtypeStruct(s_all.shape, F32),
                   jax.ShapeDtypeStruct((bs, CONV_W - 1, qkv_w), F32)],
        scratch_shapes=[pltpu.VMEM((group, 16, qkv_w), F32)],
        input_output_aliases={9: 1},
        compiler_params=_cparams(("parallel",)),
        name="delta_sample",
    )(ha3, ha3, hb3, state_conv, state_delta, w_conv, alog_vec, dtb_vec, g_norm, s_all)


def _top_blocks(gate, n_valid, axis):
    blk = lax.broadcasted_iota(jnp.int32, gate.shape, axis)
    big = jnp.int32(2 ** 30)
    cand = blk < n_valid
    g = jnp.where(cand, gate, -jnp.inf)
    picks = []
    for _ in range(MOBA_TOPK):
        mx = jnp.max(g, axis=axis, keepdims=True)
        first = jnp.min(jnp.where((g == mx) & cand, blk, big), axis=axis, keepdims=True)
        picks.append(jnp.where(first == big, -1, first))
        hit = blk == first
        cand = cand & jnp.logical_not(hit)
        g = jnp.where(hit, -jnp.inf, g)
    return picks


def _block_mean(x):
    rows = x.shape[0]
    part = jnp.sum(x.reshape(rows // 8, 8, x.shape[1]), axis=0)
    return jnp.sum(part, axis=0, keepdims=True) * (1.0 / rows)


def _stack_heads(x, heads, width):
    return jnp.concatenate([x[:, h * width:(h + 1) * width] for h in range(heads)], axis=0)


def _unstack_heads(x, heads, rows):
    return jnp.concatenate([x[h * rows:(h + 1) * rows, :] for h in range(heads)], axis=1)


def _flash_init(s, m_ref, l_ref, acc_ref, vt):
    m0 = jnp.max(s, axis=0, keepdims=True)
    p = jnp.exp(s - m0)
    m_ref[...] = m0
    l_ref[...] = jnp.sum(p, axis=0, keepdims=True)
    acc_ref[...] = _dot(vt, p.astype(BF16))


def _flash_step(blocks, m_ref, l_ref, acc_ref):
    m_old = m_ref[...]
    m_new = m_old
    for s, _ in blocks:
        m_new = jnp.maximum(m_new, jnp.max(s, axis=0, keepdims=True))
    alpha = jnp.exp(m_old - m_new)
    l_new = alpha * l_ref[...]
    acc = alpha * acc_ref[...]
    for s, vt in blocks:
        p = jnp.exp(s - m_new)
        l_new = l_new + jnp.sum(p, axis=0, keepdims=True)
        acc = acc + _dot(vt, p.astype(BF16))
    l_ref[...] = l_new
    acc_ref[...] = acc
    m_ref[...] = m_new


def _loop_in_pairs(n, block):
    def pair(jp, carry):
        block([2 * jp, 2 * jp + 1])
        return carry

    lax.fori_loop(0, lax.div(n, 2), pair, 0)

    @pl.when(lax.rem(n, 2) == 1)
    def _():
        block([n - 1])


def _causal_t(s, tq):
    kpos = lax.broadcasted_iota(jnp.int32, s.shape, 0)
    qpos = jnp.bitwise_and(lax.broadcasted_iota(jnp.int32, s.shape, 1), tq - 1)
    return jnp.where(kpos <= qpos, s, NEG)


def _moba_prompt_kernel(q_ref, k_ref, v_ref, mix_hbm, o_ref, kmean_ref, kb_ref, vt_ref, m_ref, l_ref, acc_ref, *, nblk):
    del mix_hbm
    i = pl.program_id(1)
    tq = MOBA_BLOCK

    @pl.when(i == 0)
    def _():
        kmean_ref[...] = jnp.zeros_like(kmean_ref)
        for j in range(nblk):
            kj = k_ref[j * tq:(j + 1) * tq, :]
            kmean_ref[j:j + 1, :] = _block_mean(kj)
            kb_ref[j] = kj.astype(BF16)
            vt_ref[j] = v_ref[j * tq:(j + 1) * tq, :].T.astype(BF16)

    qs = _stack_heads(q_ref[...], B_HEADS, HEAD_DIM)
    qb = (qs * (HEAD_DIM ** -0.5)).astype(BF16)
    gate = _dot_nt(kmean_ref[...], qs, precision=HIGHEST)
    picks = _top_blocks(gate, i, 0)

    _flash_init(_causal_t(_dot_nt(kb_ref[i], qb), tq), m_ref, l_ref, acc_ref, vt_ref[i])

    def past_blocks(ids):
        blocks = []
        for j in ids:
            chosen = (picks[0] == j) | (picks[1] == j) | (picks[2] == j)
            blocks.append((jnp.where(chosen, _dot_nt(kb_ref[j], qb), NEG), vt_ref[j]))
        _flash_step(blocks, m_ref, l_ref, acc_ref)

    _loop_in_pairs(i, past_blocks)
    out_t = acc_ref[...] / l_ref[...]
    for h in range(B_HEADS):
        o_ref[:, h * HEAD_DIM:(h + 1) * HEAD_DIM] = out_t[:, h * tq:(h + 1) * tq].T.astype(BF16)


def _moba_prompt(q_rot, k_rot, v, mix, bp, lp):
    tq = MOBA_BLOCK
    nq = lp // tq
    nblk = lp // MOBA_BLOCK
    nblk_pad = -(-nblk // 8) * 8
    kern = functools.partial(_moba_prompt_kernel, nblk=nblk)
    return pl.pallas_call(
        kern,
        grid=(bp, nq),
        in_specs=[pl.BlockSpec((tq, B_WIDTH), lambda b, i: (b * nq + i, 0)),
                  pl.BlockSpec((lp, HEAD_DIM), lambda b, i: (b, 0)),
                  pl.BlockSpec((lp, HEAD_DIM), lambda b, i: (b, 0)),
                  pl.BlockSpec(memory_space=pl.ANY)],
        out_specs=pl.BlockSpec((tq, B_WIDTH), lambda b, i: (b * nq + i, A_WIDTH // B_WIDTH)),
        out_shape=jax.ShapeDtypeStruct(mix.shape, BF16),
        scratch_shapes=[pltpu.VMEM((nblk_pad, HEAD_DIM), F32),
                        pltpu.VMEM((nblk, tq, HEAD_DIM), BF16),
                        pltpu.VMEM((nblk, HEAD_DIM, tq), BF16),
                        pltpu.VMEM((1, B_HEADS * tq), F32),
                        pltpu.VMEM((1, B_HEADS * tq), F32),
                        pltpu.VMEM((HEAD_DIM, B_HEADS * tq), F32)],
        input_output_aliases={3: 0},
        compiler_params=_cparams(("parallel", "arbitrary")),
        name="moba_prompt",
    )(q_rot, k_rot, v, mix)


def _mla_prompt_kernel(q_ref, kc_ref, wuv_ref, mix_hbm, o_ref, ct_ref, m_ref, l_ref, acc_ref, *, tq, nblk):
    del mix_hbm
    i = pl.program_id(1)

    @pl.when(i == 0)
    def _():
        for j in range(nblk):
            ct_ref[j] = kc_ref[j * tq:(j + 1) * tq, :C_KV_RANK].astype(F32).T.astype(BF16)

    qs = _stack_heads(q_ref[...], C_HEADS, 256)
    r0 = pl.multiple_of(i * tq, tq)
    _flash_init(_causal_t(_dot_nt(kc_ref[pl.ds(r0, tq), :], qs), tq), m_ref, l_ref, acc_ref, ct_ref[i])

    def past_blocks(ids):
        blocks = []
        for j in ids:
            c0 = pl.multiple_of(j * tq, tq)
            blocks.append((_dot_nt(kc_ref[pl.ds(c0, tq), :], qs), ct_ref[j]))
        _flash_step(blocks, m_ref, l_ref, acc_ref)

    _loop_in_pairs(i, past_blocks)
    o_lat_t = (acc_ref[...] / l_ref[...]).astype(BF16)
    for h in range(C_HEADS):
        o_ref[:, h * C_V:(h + 1) * C_V] = _dot_tn(o_lat_t[:, h * tq:(h + 1) * tq], wuv_ref[h]).astype(BF16)


def _mla_prompt(qcat, kcat, w_uv, mix, layer, bp, lp, tq):
    nq = lp // tq
    kern = functools.partial(_mla_prompt_kernel, tq=tq, nblk=nq)
    return pl.pallas_call(
        kern,
        grid=(bp, nq),
        in_specs=[pl.BlockSpec((tq, 1024), lambda b, i: (b * nq + i, 0)),
                  pl.BlockSpec((lp, 256), lambda b, i: (b, 0)),
                  pl.BlockSpec((None, C_HEADS, C_KV_RANK, C_V), lambda b, i: (layer, 0, 0, 0)),
                  pl.BlockSpec(memory_space=pl.ANY)],
        out_specs=pl.BlockSpec((tq, C_WIDTH), lambda b, i: (b * nq + i, (A_WIDTH + B_WIDTH) // C_WIDTH)),
        out_shape=jax.ShapeDtypeStruct(mix.shape, BF16),
        scratch_shapes=[pltpu.VMEM((nq, C_KV_RANK, tq), BF16),
                        pltpu.VMEM((1, C_HEADS * tq), F32),
                        pltpu.VMEM((1, C_HEADS * tq), F32),
                        pltpu.VMEM((C_KV_RANK, C_HEADS * tq), F32)],
        input_output_aliases={3: 0},
        compiler_params=_cparams(("parallel", "arbitrary")),
        name="mla_prompt",
    )(qcat, kcat, w_uv, mix)


def _attn_sample_kernel(pt_ref, qm_ref, knew_ref, vnew_ref, qc_ref, kcnew_ref, wuv_ref,
                        ck_hbm, cv_hbm, cc_hbm, cr_hbm,
                        yb_ref, yc_ref,
                        kbuf, vbuf, cbuf, rbuf, sems, kmean_ref, sm_ref, sc_ref,
                        *, n_pages, page_off, ls):
    b = pl.program_id(0)
    nbatch = pl.num_programs(0)
    slot = lax.rem(b, 2)
    n_past = n_pages * PAGE_SIZE
    nblk = n_past // MOBA_BLOCK
    ppb = MOBA_BLOCK // PAGE_SIZE
    rows = B_HEADS * ls

    def page_copies(bb, p, sl):
        pg = pt_ref[bb, p] + page_off
        dst = pl.ds(pl.multiple_of(p * PAGE_SIZE, PAGE_SIZE), PAGE_SIZE)
        return (pltpu.make_async_copy(ck_hbm.at[pg], kbuf.at[sl, dst, :], sems.at[0, sl]),
                pltpu.make_async_copy(cv_hbm.at[pg], vbuf.at[sl, dst, :], sems.at[1, sl]),
                pltpu.make_async_copy(cc_hbm.at[pg], cbuf.at[sl, dst, :], sems.at[2, sl]),
                pltpu.make_async_copy(cr_hbm.at[pg], rbuf.at[sl, p], sems.at[3, sl]))

    unroll = 4 if n_pages % 4 == 0 else 1

    def start_batch(bb, sl):
        def body(p, carry):
            for cp in page_copies(bb, p, sl):
                cp.start()
            return carry
        lax.fori_loop(0, n_pages, body, 0, unroll=unroll)

    def wait_batch(bb, sl):
        def body(p, carry):
            for cp in page_copies(bb, p, sl):
                cp.wait()
            return carry
        lax.fori_loop(0, n_pages, body, 0, unroll=unroll)

    @pl.when(b == 0)
    def _():
        start_batch(b, slot)

    @pl.when(b + 1 < nbatch)
    def _():
        start_batch(b + 1, 1 - slot)

    wait_batch(b, slot)

    bpc = 4 if nblk % 4 == 0 else (2 if nblk % 2 == 0 else 1)
    chunk = bpc * MOBA_BLOCK
    nchunk = n_past // chunk

    rtok = lax.rem(lax.broadcasted_iota(jnp.int32, (rows, ls), 0), ls)
    ctok = lax.broadcasted_iota(jnp.int32, (rows, ls), 1)
    causal_new = ctok <= rtok

    qs = _stack_heads(qm_ref[...], B_HEADS, HEAD_DIM)
    qb = (qs * (HEAD_DIM ** -0.5)).astype(BF16)
    kmean_ref[...] = jnp.zeros_like(kmean_ref)
    for c in range(nchunk):
        kc = kbuf[slot, c * chunk:(c + 1) * chunk, :]
        for t in range(bpc):
            j = c * bpc + t
            kmean_ref[j:j + 1, :] = _block_mean(kc[t * MOBA_BLOCK:(t + 1) * MOBA_BLOCK, :])
        sm_ref[:, c * chunk:(c + 1) * chunk] = _dot_nt(qb, kc.astype(BF16))
    gate = _dot_nt(qs, kmean_ref[...], precision=HIGHEST)
    picks = _top_blocks(gate, nblk, 1)
    for j in range(nblk):
        chosen = (picks[0] == j) | (picks[1] == j) | (picks[2] == j)
        blk = slice(j * MOBA_BLOCK, (j + 1) * MOBA_BLOCK)
        sm_ref[:, blk] = jnp.where(chosen, sm_ref[:, blk], NEG)
    s_new = jnp.where(causal_new, _dot_nt(qb, knew_ref[...].astype(BF16)), NEG)
    s_all = sm_ref[...]
    m = jnp.maximum(jnp.max(s_all, axis=1, keepdims=True), jnp.max(s_new, axis=1, keepdims=True))
    p_new = jnp.exp(s_new - m)
    sm_ref[...] = jnp.exp(s_all - m)
    den = jnp.sum(sm_ref[...], axis=1, keepdims=True) + jnp.sum(p_new, axis=1, keepdims=True)
    acc = _dot(p_new.astype(BF16), vnew_ref[...].astype(BF16))
    for c in range(nchunk):
        vc = vbuf[slot, c * chunk:(c + 1) * chunk, :].astype(BF16)
        acc = acc + _dot(sm_ref[:, c * chunk:(c + 1) * chunk].astype(BF16), vc)
    yb_ref[...] = _unstack_heads(acc / den, B_HEADS, ls).astype(BF16)

    qc = _stack_heads(qc_ref[...], C_HEADS, 256)
    ql = qc[:, :C_KV_RANK]
    qr = qc[:, C_KV_RANK:C_KV_RANK + C_ROPE]
    ppc = chunk // PAGE_SIZE
    for c in range(nchunk):
        cc = cbuf[slot, c * chunk:(c + 1) * chunk, :].astype(BF16)
        rc = jnp.concatenate([rbuf[slot, c * ppc + t] for t in range(ppc)], axis=1).astype(BF16)
        sc_ref[:, c * chunk:(c + 1) * chunk] = _dot_nt(ql, cc) + _dot(qr, rc)
    kcn = kcnew_ref[...]
    s_new = jnp.where(causal_new, _dot_nt(qc, kcn), NEG)
    s_all = sc_ref[...]
    m = jnp.maximum(jnp.max(s_all, axis=1, keepdims=True), jnp.max(s_new, axis=1, keepdims=True))
    p_new = jnp.exp(s_new - m)
    sc_ref[...] = jnp.exp(s_all - m)
    den = jnp.sum(sc_ref[...], axis=1, keepdims=True) + jnp.sum(p_new, axis=1, keepdims=True)
    acc = _dot(p_new.astype(BF16), kcn[:, :C_KV_RANK])
    for c in range(nchunk):
        cc = cbuf[slot, c * chunk:(c + 1) * chunk, :].astype(BF16)
        acc = acc + _dot(sc_ref[:, c * chunk:(c + 1) * chunk].astype(BF16), cc)
    o_lat = (acc / den).astype(BF16)
    for h in range(C_HEADS):
        yc_ref[:, h * C_V:(h + 1) * C_V] = _dot(o_lat[h * ls:(h + 1) * ls, :], wuv_ref[h]).astype(BF16)


def _attn_sample(page_table, q_rot3, k_rot3, v3, qcat3, kcat3, w_uv, caches, layer, n_pool, bs, ls):
    n_pages = page_table.shape[1]
    n_past = n_pages * PAGE_SIZE
    nblk = n_past // MOBA_BLOCK
    nblk_pad = -(-nblk // 8) * 8
    rows = B_HEADS * ls
    kern = functools.partial(_attn_sample_kernel, n_pages=n_pages, page_off=layer * n_pool, ls=ls)
    new = lambda w: pl.BlockSpec((None, ls, w), lambda b, pt: (b, 0, 0))
    grid_spec = pltpu.PrefetchScalarGridSpec(
        num_scalar_prefetch=1,
        grid=(bs,),
        in_specs=[new(B_WIDTH), new(HEAD_DIM), new(HEAD_DIM), new(1024), new(256),
                  pl.BlockSpec((None, C_HEADS, C_KV_RANK, C_V), lambda b, pt: (layer, 0, 0, 0)),
                  pl.BlockSpec(memory_space=pl.ANY), pl.BlockSpec(memory_space=pl.ANY),
                  pl.BlockSpec(memory_space=pl.ANY), pl.BlockSpec(memory_space=pl.ANY)],
        out_specs=[new(B_WIDTH), new(C_WIDTH)],
        scratch_shapes=[pltpu.VMEM((2, n_past, HEAD_DIM), F32),
                        pltpu.VMEM((2, n_past, HEAD_DIM), F32),
                        pltpu.VMEM((2, n_past, C_KV_RANK), F32),
                        pltpu.VMEM((2, n_pages, C_ROPE, PAGE_SIZE), F32),
                        pltpu.SemaphoreType.DMA((4, 2)),
                        pltpu.VMEM((nblk_pad, HEAD_DIM), F32),
                        pltpu.VMEM((rows, n_past), F32),
                        pltpu.VMEM((rows, n_past), F32)])
    return pl.pallas_call(
        kern,
        grid_spec=grid_spec,
        out_shape=[jax.ShapeDtypeStruct((bs, ls, B_WIDTH), BF16),
                   jax.ShapeDtypeStruct((bs, ls, C_WIDTH), BF16)],
        compiler_params=_cparams(("arbitrary",)),
        name="attn_sample",
    )(page_table, q_rot3, k_rot3, v3, qcat3, kcat3, w_uv, *caches)


def _rope_tables(pos, d, reps):
    half = d // 2
    inv = ROPE_THETA ** (-jnp.arange(half, dtype=F32) * 2.0 / d)
    ang = pos.astype(F32)[:, None] * inv[None, :]
    cos = jnp.cos(ang)
    sin = jnp.sin(ang)
    return (jnp.tile(jnp.concatenate([cos, cos], axis=1), (1, reps)),
            jnp.tile(jnp.concatenate([-sin, sin], axis=1), (1, reps)))


def _lane_vec(v, lane0):
    depth, heads = v.shape
    out = jnp.zeros((depth, 1, 128), F32)
    return out.at[:, 0, lane0:lane0 + heads].set(v.astype(F32))


def kernel(x_prompt, x_sample, cache_moba_k, cache_moba_v, cache_mla_ckv, cache_mla_krope, state_delta, state_conv, page_table, w_in, w_conv, a_log, dt_bias, g_norm_a, g_q, w_uq, g_kv, w_uk, w_uv, w_o, ln1_g, ln1_b, w_ffn_in, w_ffn_out, ln2_g, ln2_b):
    bp, lp, d = x_prompt.shape
    bs, ls, _ = x_sample.shape
    depth = w_in.shape[0]
    n_pool = cache_moba_k.shape[1]
    n_pages = page_table.shape[1]
    n_past = n_pages * PAGE_SIZE
    n_p = bp * lp
    n_s = bs * ls
    n = n_p + n_s
    assert d == D_MODEL and lp % MOBA_BLOCK == 0 and n_past % MOBA_BLOCK == 0 and ls <= 8
    alpha = (2 * depth) ** 0.25

    tm = math.gcd(math.gcd(n_p, n_s), 512)
    tile_a = 256 if lp % 256 == 0 else DELTA_SUB
    tm_proj = n // 8 if n % 128 == 0 else tm

    o_ba = HA_WIDTH
    o_rest = o_ba + 2 * A_HEADS
    n_rest = w_in.shape[-1] - o_rest
    assert n_rest == OFF_KR + C_ROPE
    w_in_a = w_in[..., :HA_WIDTH].astype(BF16)
    w_in_b = jnp.concatenate(
        [w_in[..., o_rest:], w_in[..., o_ba:o_rest],
         jnp.zeros(w_in.shape[:2] + (HB_WIDTH - n_rest - 2 * A_HEADS,), w_in.dtype)], axis=-1).astype(BF16)
    w_uq_p = jnp.concatenate([w_uq[..., :C_NOPE].reshape(depth, C_Q_RANK, C_HEADS * C_NOPE),
                              w_uq[..., C_NOPE:].reshape(depth, C_Q_RANK, C_HEADS * C_ROPE)], axis=-1).astype(BF16)
    w_ukt = jnp.transpose(w_uk, (0, 2, 3, 1)).astype(BF16)
    w_uv_p = jnp.transpose(w_uv, (0, 2, 1, 3)).astype(BF16)
    w_o_b = w_o.astype(BF16)
    w_ffn_in_b = w_ffn_in.astype(BF16)
    w_ffn_out_b = w_ffn_out.astype(BF16)
    alog_vec = _lane_vec(a_log, LANE_DECAY)
    dtb_vec = _lane_vec(dt_bias, LANE_DECAY)
    g_norm3 = g_norm_a.reshape(depth, 1, HEAD_DIM)
    g_q3 = g_q.reshape(depth, 1, C_Q_RANK)
    g_kv3 = g_kv.reshape(depth, 1, C_KV_RANK)
    ln1_g3, ln1_b3 = ln1_g.reshape(depth, 1, d), ln1_b.reshape(depth, 1, d)
    ln2_g3, ln2_b3 = ln2_g.reshape(depth, 1, d), ln2_b.reshape(depth, 1, d)

    pos = jnp.concatenate([jnp.tile(jnp.arange(lp, dtype=jnp.int32), bp),
                           jnp.tile(n_past + jnp.arange(ls, dtype=jnp.int32), bs)])
    tabs = _rope_tables(pos, HEAD_DIM, 1) + _rope_tables(pos, C_ROPE, 4)

    caches = (cache_moba_k.reshape(depth * n_pool, PAGE_SIZE, HEAD_DIM),
              cache_moba_v.reshape(depth * n_pool, PAGE_SIZE, HEAD_DIM),
              cache_mla_ckv.reshape(depth * n_pool, PAGE_SIZE, C_KV_RANK),
              jnp.swapaxes(cache_mla_krope, 2, 3).reshape(depth * n_pool, C_ROPE, PAGE_SIZE))

    x = jnp.concatenate([x_prompt.reshape(n_p, d), x_sample.reshape(n_s, d)], axis=0)
    xb = x.astype(BF16)
    mix = jnp.zeros((n, MIX_WIDTH), BF16)
    s_all = jnp.zeros(state_delta.shape, F32)
    rec = []
    for layer in range(depth):
        ha = _matmul(xb, w_in_a, layer, tm_proj, HA_WIDTH // 4)
        hb = _matmul(xb, w_in_b, layer, tm_proj, HB_WIDTH)
        q_rot, k_rot, ckv_n, kr_rot, qcat, kcat = _prep(hb, tabs, g_q3, g_kv3, w_uq_p, w_ukt, layer, tm)
        vb = hb[:, OFF_VB:OFF_VB + HEAD_DIM]

        mix, s_p, conv_p = _delta_prompt(ha, hb, mix, w_conv, alog_vec, dtb_vec, g_norm3, layer, bp, lp, tile_a)
        mix = _moba_prompt(q_rot, k_rot, vb, mix, bp, lp)
        mix = _mla_prompt(qcat, kcat, w_uv_p, mix, layer, bp, lp, MOBA_BLOCK)

        samp = lambda a: a[n_p:].reshape(bs, ls, a.shape[-1])
        ya_s, s_all, conv_s = _delta_sample(samp(ha), samp(hb), state_conv, state_delta, s_all, w_conv, alog_vec, dtb_vec,
                                            g_norm3, layer, bs, ls)
        yb_s, yc_s = _attn_sample(page_table, samp(q_rot), samp(k_rot), samp(vb), samp(qcat), samp(kcat),
                                  w_uv_p, caches, layer, n_pool, bs, ls)
        mix_s = jnp.concatenate([ya_s.reshape(n_s, A_WIDTH), yb_s.reshape(n_s, B_WIDTH), yc_s.reshape(n_s, C_WIDTH)],
                                axis=1)
        mix = lax.dynamic_update_slice(mix, mix_s, (n_p, 0))

        x1, x1b = _outproj_ln(mix, w_o_b, x, ln1_g3, ln1_b3, layer, alpha, tm)
        x, xb = _ffn_ln(x1b, x1, w_ffn_in_b, w_ffn_out_b, ln2_g3, ln2_b3, layer, alpha, tm, 512)
        rec.append((k_rot, vb, ckv_n, kr_rot, s_p, conv_p, conv_s))

    def stack(idx, lo, hi, shape):
        return jnp.stack([r[idx][lo:hi].reshape(shape) for r in rec])

    return (x[:n_p].reshape(bp, lp, d), x[n_p:].reshape(bs, ls, d),
            stack(0, 0, n_p, (bp, lp, 1, HEAD_DIM)), stack(1, 0, n_p, (bp, lp, 1, HEAD_DIM)),
            stack(2, 0, n_p, (bp, lp, C_KV_RANK)), stack(3, 0, n_p, (bp, lp, C_ROPE)),
            jnp.stack([r[4] for r in rec]), jnp.stack([r[5] for r in rec]),
            stack(0, n_p, n, (bs, ls, 1, HEAD_DIM)), stack(1, n_p, n, (bs, ls, 1, HEAD_DIM)),
            stack(2, n_p, n, (bs, ls, C_KV_RANK)), stack(3, n_p, n, (bs, ls, C_ROPE)),
            s_all, jnp.stack([r[6] for r in rec]))
```
